```python
import math
import jax, jax.numpy as jnp
from jax import lax
import numpy as np

D_MODEL = 1024
BATCH = 8
SEQ = 2048
DEPTH = 1
DEC_BATCH = 128
DEC_SEQ = 1
PAST_LEN = 16384
PAGE_SIZE = 128

N_META = 16
DN_HEADS = 4
DN_DK = 128
DN_DV = 128
DN_QK_W = DN_HEADS * DN_DK
DN_V_W = DN_HEADS * DN_DV
DN_CONV = 4
DN_CHUNK = 64
RW_HEADS = 8
RW_N = 64
RW_W = RW_HEADS * RW_N
RW_W_RANK = 64
RW_A_RANK = 64
RW_G_RANK = 128
PEER_HEADS = 8
PEER_NKEYS = 128
PEER_N = PEER_NKEYS * PEER_NKEYS
PEER_QDIM = 256
PEER_HALF = PEER_QDIM // 2
PEER_TOPK = 16
PEER_BLOCK = 256
NORM_EPS = 1e-6
GN_EPS = 64e-5

N_QKV = 2 * DN_QK_W + DN_V_W
O_Z = N_QKV
O_BETA = O_Z + DN_V_W
O_ALPHA = O_BETA + DN_HEADS
O_RW = O_ALPHA + DN_HEADS
N_RW = 3 * RW_W + RW_W_RANK + RW_A_RANK + RW_G_RANK
O_GATE = O_RW + N_RW
N_IN = O_GATE + 2 * D_MODEL

kernel_name = 'hybrid_deltanet_rwkv7_peer_meta_step'


def rmsnorm(x, g):
    xf = x.astype(jnp.float32)
    y = xf * lax.rsqrt(jnp.mean(xf * xf, -1, keepdims=True) + NORM_EPS) * g.astype(jnp.float32)
    return y.astype(x.dtype)


def l2norm(x):
    xf = x.astype(jnp.float32)
    return xf * lax.rsqrt(jnp.sum(xf * xf, -1, keepdims=True) + 1e-6)


def short_conv(u, buf, w):
    T = u.shape[1]
    up = jnp.concatenate([buf.astype(u.dtype), u], axis=1)
    y = up[:, 0:T] * w[0]
    for j in range(1, DN_CONV):
        y = y + up[:, j:j + T] * w[j]
    return jax.nn.silu(y), up[:, T:]


def dn_step(S, inp):
    q, k, v, g, beta = inp
    S = S * jnp.exp(g)[..., None, None]
    kv = jnp.einsum('bhkv,bhk->bhv', S, k)
    delta = (v - kv) * beta[..., None]
    S = S + k[..., :, None] * delta[..., None, :]
    return S, jnp.einsum('bhkv,bhk->bhv', S, q)


def dn_chunk(S, q, k, v, g, beta):
    L = q.shape[2]
    G = jnp.cumsum(g, -1)
    tri = jnp.tril(jnp.ones((L, L), bool))
    strict = jnp.tril(jnp.ones((L, L), bool), -1)
    decay = jnp.exp(jnp.where(tri, G[..., :, None] - G[..., None, :], -jnp.inf))
    kb = k * beta[..., None]
    A = jnp.where(strict, jnp.einsum('bhik,bhjk->bhij', kb, k) * decay, 0.0) + jnp.eye(L, dtype=q.dtype)
    rhs = jnp.concatenate([v * beta[..., None], kb * jnp.exp(G)[..., None]], -1)
    sol = lax.linalg.triangular_solve(A, rhs, left_side=True, lower=True, unit_diagonal=True)
    u, w = sol[..., :DN_DV], sol[..., DN_DV:]
    v_new = u - jnp.einsum('bhlk,bhkv->bhlv', w, S)
    qk = jnp.where(tri, jnp.einsum('bhik,bhjk->bhij', q, k) * decay, 0.0)
    o = jnp.einsum('bhlk,bhkv->bhlv', q * jnp.exp(G)[..., None], S) + jnp.einsum('bhij,bhjv->bhiv', qk, v_new)
    G_last = G[..., -1:]
    S = S * jnp.exp(G_last)[..., None] + jnp.einsum('bhlk,bhlv->bhkv', k * jnp.exp(G_last - G)[..., None], v_new)
    return S, o


def dn_chunked(S, q, k, v, g, beta, lead):
    B, H, T = g.shape
    S, o_lead = dn_chunk(S, q[:, :, :lead], k[:, :, :lead], v[:, :, :lead], g[:, :, :lead], beta[:, :, :lead])
    nc = (T - lead) // DN_CHUNK

    def split(a):
        a = a[:, :, lead:]
        return jnp.moveaxis(a.reshape((B, H, nc, DN_CHUNK) + a.shape[3:]), 2, 0)

    S, o_rest = lax.scan(lambda s, c: dn_chunk(s, *c), S, tuple(split(a) for a in (q, k, v, g, beta)))
    o_rest = jnp.moveaxis(o_rest, 0, 2).reshape(B, H, nc * DN_CHUNK, DN_DV)
    return S, jnp.concatenate([o_lead, o_rest], axis=2)


def rw_step(S, inp):
    r, w, k, v, a, b = inp
    sa = jnp.einsum('bhvk,bhk->bhv', S, a)
    S = S * w[:, :, None, :] + sa[..., :, None] * b[..., None, :] + v[..., :, None] * k[..., None, :]
    return S, jnp.einsum('bhvk,bhk->bhv', S, r)


def token_mixer(xn, conv_buf, s_dn, shift_buf, s_rw, p, lead):
    B, T, _ = xn.shape
    f32 = jnp.float32
    P = xn @ p['w_in']
    qkv, conv_new = short_conv(P[..., :N_QKV], conv_buf, p['dn_conv_w'])
    q = l2norm(qkv[..., :DN_QK_W].reshape(B, T, DN_HEADS, DN_DK)) * (DN_DK ** -0.5)
    k = l2norm(qkv[..., DN_QK_W:2 * DN_QK_W].reshape(B, T, DN_HEADS, DN_DK))
    v = qkv[..., 2 * DN_QK_W:].reshape(B, T, DN_HEADS, DN_DV).astype(f32)
    beta = jax.nn.sigmoid(P[..., O_BETA:O_ALPHA].astype(f32))
    g = -jnp.exp(p['dn_a_log'].astype(f32)) * jax.nn.softplus(P[..., O_ALPHA:O_RW].astype(f32) + p['dn_dt_bias'])
    S0 = s_dn.astype(f32)
    if lead is None:
        tm = lambda a: jnp.moveaxis(a, 1, 0)
        s_dn_new, o = lax.scan(dn_step, S0, (tm(q), tm(k), tm(v), tm(g), tm(beta)))
        o = jnp.moveaxis(o, 0, 1)
    else:
        hm = lambda a: jnp.moveaxis(a, 2, 1)
        s_dn_new, o = dn_chunked(S0, hm(q), hm(k), hm(v), hm(g), hm(beta), lead)
        o = jnp.moveaxis(o, 1, 2)
    z = P[..., O_Z:O_BETA].reshape(B, T, DN_HEADS, DN_DV).astype(f32)
    o_dn = (rmsnorm(o, p['dn_norm_w']) * jax.nn.silu(z)).reshape(B, T, DN_V_W).astype(xn.dtype)
    prw = P[..., O_RW:O_GATE]
    prev = jnp.concatenate([shift_buf.astype(prw.dtype), prw[:, :-1]], axis=1)
    mix = (prw + (prev - prw) * p['rw_mu']).astype(f32)
    shift_new = prw[:, -1:]
    c0 = 3 * RW_W
    r = mix[..., :RW_W]
    kr = mix[..., RW_W:2 * RW_W]
    vr = mix[..., 2 * RW_W:c0]
    wl = mix[..., c0:c0 + RW_W_RANK]
    al = mix[..., c0 + RW_W_RANK:c0 + RW_W_RANK + RW_A_RANK]
    gl = mix[..., c0 + RW_W_RANK + RW_A_RANK:]
    w_log = -jax.nn.softplus(-(p['rw_w0'] + jnp.tanh(wl) @ p['rw_w2'])) - 0.5
    decay = jnp.exp(-jnp.exp(w_log))
    a = jax.nn.sigmoid(p['rw_a0'] + al @ p['rw_a2'])
    gate = jax.nn.sigmoid(gl) @ p['rw_g2']
    hs = lambda t: t.reshape(B, T, RW_HEADS, RW_N)
    kk = l2norm(hs(kr * p['rw_k_k']))
    kr = kr * (1.0 + (a - 1.0) * p['rw_k_a'])
    r_h, k_h, v_h, a_h, w_h = hs(r), hs(kr), hs(vr), hs(a), hs(decay)
    tm = lambda t: jnp.moveaxis(t, 1, 0)
    s_rw_new, y = lax.scan(rw_step, s_rw.astype(f32), (tm(r_h), tm(w_h), tm(k_h), tm(v_h), tm(-kk), tm(kk * a_h)))
    y = jnp.moveaxis(y, 0, 1)
    mu = jnp.mean(y, -1, keepdims=True)
    var = jnp.mean(jnp.square(y - mu), -1, keepdims=True)
    y = ((y - mu) * lax.rsqrt(var + GN_EPS)).reshape(B, T, RW_W) * p['rw_gn_w'] + p['rw_gn_b']
    bonus = (jnp.sum(r_h * k_h * p['rw_r_k'], -1, keepdims=True) * v_h).reshape(B, T, RW_W)
    o_rw = ((y + bonus) * gate).astype(xn.dtype)
    ga = jax.nn.sigmoid(P[..., O_GATE:O_GATE + D_MODEL])
    gb = jax.nn.sigmoid(P[..., O_GATE + D_MODEL:])
    m = ga * (o_dn @ p['w_up_dn']) + gb * (o_rw @ p['w_up_rw'])
    out = m @ p['w_out']
    return out, (conv_new, s_dn_new.astype(s_dn.dtype), shift_new, s_rw_new.astype(s_rw.dtype))


def peer_ffn(x, wq, keys, u_tab, v_tab):
    M, D = x.shape
    nb = -(-M // PEER_BLOCK)
    xp = jnp.pad(x, ((0, nb * PEER_BLOCK - M), (0, 0))).reshape(nb, PEER_BLOCK, D)
    K = PEER_TOPK

    def block(xb):
        q = (xb @ wq).reshape(PEER_BLOCK, PEER_HEADS, 2, PEER_HALF)
        s = jnp.einsum('thpc,hpnc->thpn', q, keys)
        sv, si = lax.top_k(s, K)
        cand = (sv[:, :, 0, :, None] + sv[:, :, 1, None, :]).reshape(PEER_BLOCK, PEER_HEADS, K * K)
        cidx = (si[:, :, 0, :, None] * PEER_NKEYS + si[:, :, 1, None, :]).reshape(PEER_BLOCK, PEER_HEADS, K * K)
        cv, ci = lax.top_k(cand, K)
        eidx = jnp.take_along_axis(cidx, ci, -1)
        gates = jax.nn.softmax(cv.astype(jnp.float32), -1)
        h = jax.nn.gelu(jnp.einsum('thkd,td->thk', u_tab[eidx], xb).astype(jnp.float32))
        return jnp.einsum('thk,thkd->td', (gates * h).astype(xb.dtype), v_tab[eidx])

    return lax.map(block, xp).reshape(nb * PEER_BLOCK, D)[:M]


def layer(h, states, p, lead):
    conv_buf, s_dn, shift_buf, s_rw = states
    mix, new = token_mixer(rmsnorm(h, p['ln1']), conv_buf, s_dn, shift_buf, s_rw, p, lead)
    h = h + mix.astype(h.dtype)
    B, T, D = h.shape
    f = peer_ffn(rmsnorm(h, p['ln2']).reshape(B * T, D), p['peer_wq'], p['peer_keys'], p['peer_u'], p['peer_v'])
    return h + f.reshape(B, T, D).astype(h.dtype), new


def setup_inputs(seed: int = 0) -> dict:
    key = jax.random.key(seed)
    ks = iter(jax.random.split(key, 48))
    nrm = lambda shape, scale: jax.random.normal(next(ks), shape, jnp.float32) * scale
    L = DEPTH
    d = {}
    d['x_prompt'] = nrm((BATCH, SEQ, D_MODEL), 1.0)
    d['x_sample'] = nrm((DEC_BATCH, DEC_SEQ, D_MODEL), 1.0)
    d['cache_dn_conv'] = nrm((L, DEC_BATCH, DN_CONV - 1, N_QKV), 1.0)
    d['state_dn'] = nrm((L, DEC_BATCH, DN_HEADS, DN_DK, DN_DV), 0.3)
    d['cache_rw_shift'] = nrm((L, DEC_BATCH, 1, N_RW), 1.0)
    d['state_rw'] = nrm((L, DEC_BATCH, RW_HEADS, RW_N, RW_N), 1.0)
    d['meta_tokens'] = nrm((N_META, D_MODEL), 1.0)
    d['ln1'] = 1.0 + nrm((L, D_MODEL), 0.02)
    d['w_in'] = nrm((L, D_MODEL, N_IN), D_MODEL ** -0.5)
    d['dn_conv_w'] = nrm((L, DN_CONV, N_QKV), DN_CONV ** -0.5)
    d['dn_a_log'] = jnp.log(jax.random.uniform(next(ks), (L, DN_HEADS), jnp.float32, 1.0, 16.0))
    dt = jnp.exp(jax.random.uniform(next(ks), (L, DN_HEADS), jnp.float32, math.log(1e-3), math.log(1e-1)))
    d['dn_dt_bias'] = dt + jnp.log(-jnp.expm1(-dt))
    d['dn_norm_w'] = 1.0 + nrm((L, DN_DV), 0.02)
    d['rw_mu'] = jax.random.uniform(next(ks), (L, N_RW), jnp.float32, 0.0, 1.0)
    d['rw_w0'] = jax.random.uniform(next(ks), (L, RW_W), jnp.float32, -6.0, 1.0)
    d['rw_w2'] = nrm((L, RW_W_RANK, RW_W), 0.5 * RW_W_RANK ** -0.5)
    d['rw_a0'] = nrm((L, RW_W), 0.1)
    d['rw_a2'] = nrm((L, RW_A_RANK, RW_W), RW_A_RANK ** -0.5)
    d['rw_g2'] = nrm((L, RW_G_RANK, RW_W), RW_G_RANK ** -0.5)
    d['rw_k_k'] = 0.85 + nrm((L, RW_W), 0.02)
    d['rw_k_a'] = 1.0 + nrm((L, RW_W), 0.02)
    d['rw_r_k'] = nrm((L, RW_HEADS, RW_N), 0.1)
    d['rw_gn_w'] = 1.0 + nrm((L, RW_W), 0.02)
    d['rw_gn_b'] = nrm((L, RW_W), 0.02)
    d['w_up_dn'] = nrm((L, DN_V_W, D_MODEL), DN_V_W ** -0.5)
    d['w_up_rw'] = nrm((L, RW_W, D_MODEL), RW_W ** -0.5)
    d['w_out'] = nrm((L, D_MODEL, D_MODEL), D_MODEL ** -0.5)
    d['ln2'] = 1.0 + nrm((L, D_MODEL), 0.02)
    d['peer_wq'] = nrm((L, D_MODEL, PEER_HEADS * PEER_QDIM), D_MODEL ** -0.5)
    d['peer_keys'] = nrm((L, PEER_HEADS, 2, PEER_NKEYS, PEER_HALF), PEER_HALF ** -0.5)
    d['peer_u'] = nrm((L, PEER_N, D_MODEL), D_MODEL ** -0.5)
    d['peer_v'] = nrm((L, PEER_N, D_MODEL), PEER_HEADS ** -0.5)
    d['ln_f'] = 1.0 + nrm((D_MODEL,), 0.02)
    return d


def reference(x_prompt, x_sample, cache_dn_conv, state_dn, cache_rw_shift, state_rw, meta_tokens, ln1, w_in, dn_conv_w, dn_a_log, dn_dt_bias, dn_norm_w, rw_mu, rw_w0, rw_w2, rw_a0, rw_a2, rw_g2, rw_k_k, rw_k_a, rw_r_k, rw_gn_w, rw_gn_b, w_up_dn, w_up_rw, w_out, ln2, peer_wq, peer_keys, peer_u, peer_v, ln_f):
    Bp = x_prompt.shape[0]
    dt = x_prompt.dtype
    hp = jnp.concatenate([jnp.broadcast_to(meta_tokens.astype(dt), (Bp, N_META, D_MODEL)), x_prompt], axis=1)
    hs = x_sample
    zero_states = (jnp.zeros((Bp, DN_CONV - 1, N_QKV), dt), jnp.zeros((Bp, DN_HEADS, DN_DK, DN_DV), dt),
                   jnp.zeros((Bp, 1, N_RW), dt), jnp.zeros((Bp, RW_HEADS, RW_N, RW_N), dt))
    new_p, new_s = [], []
    for l in range(DEPTH):
        p = dict(ln1=ln1[l], w_in=w_in[l], dn_conv_w=dn_conv_w[l], dn_a_log=dn_a_log[l], dn_dt_bias=dn_dt_bias[l],
                 dn_norm_w=dn_norm_w[l], rw_mu=rw_mu[l], rw_w0=rw_w0[l], rw_w2=rw_w2[l], rw_a0=rw_a0[l],
                 rw_a2=rw_a2[l], rw_g2=rw_g2[l], rw_k_k=rw_k_k[l], rw_k_a=rw_k_a[l], rw_r_k=rw_r_k[l],
                 rw_gn_w=rw_gn_w[l], rw_gn_b=rw_gn_b[l], w_up_dn=w_up_dn[l], w_up_rw=w_up_rw[l], w_out=w_out[l],
                 ln2=ln2[l], peer_wq=peer_wq[l], peer_keys=peer_keys[l], peer_u=peer_u[l], peer_v=peer_v[l])
        hp, st_p = layer(hp, zero_states, p, N_META)
        hs, st_s = layer(hs, (cache_dn_conv[l], state_dn[l], cache_rw_shift[l], state_rw[l]), p, None)
        new_p.append(st_p)
        new_s.append(st_s)
    y_prompt = rmsnorm(hp[:, N_META:], ln_f)
    y_sample = rmsnorm(hs, ln_f)
    p_conv, p_dn, p_shift, p_rw = [jnp.stack(t) for t in zip(*new_p)]
    s_conv, s_dn, s_shift, s_rw = [jnp.stack(t) for t in zip(*new_s)]
    return (y_prompt, y_sample, p_conv, p_dn, p_shift, p_rw, s_conv, s_dn, s_shift, s_rw)
```

```python
import functools

import jax
import jax.numpy as jnp
from jax import lax
from jax.experimental import pallas as pl
from jax.experimental.pallas import tpu as pltpu

F32 = jnp.float32
BF16 = jnp.bfloat16

NORM_EPS = 1e-6
L2_EPS = 1e-6
GN_EPS = 64e-5
PEER_TOPK = 16

LANES = 128
CHUNK = 64
ROWS = 2 * CHUNK
INV_BLOCK = 16
VMEM_LIMIT = 52 * 1024 * 1024

NN = ((1,), (0,))
NT = ((1,), (1,))
TN = ((0,), (0,))


def _dg(a, b, dims=NN):
    return lax.dot_general(a, b, (dims, ((), ())), preferred_element_type=F32)


def _mm1(a, b, dims=NN):
    return _dg(a.astype(BF16), b.astype(BF16), dims)


def _hi_lo(x):
    hi = x.astype(BF16)
    lo = (x - hi.astype(F32)).astype(BF16)
    return hi, lo


def _mm3(a, b, dims=NN):
    a1, a2 = _hi_lo(a)
    b1, b2 = _hi_lo(b)
    return _dg(a1, b1, dims) + (_dg(a1, b2, dims) + _dg(a2, b1, dims))


def _split3(x):
    x1 = x.astype(BF16)
    r1 = x - x1.astype(F32)
    x2 = r1.astype(BF16)
    x3 = (r1 - x2.astype(F32)).astype(BF16)
    return x1, x2, x3


def _mm_mask_lhs(m, x, dims=NN):
    x1, x2, x3 = _split3(x)
    m = m.astype(BF16)
    return _dg(m, x1, dims) + (_dg(m, x2, dims) + _dg(m, x3, dims))


def _mm_mask_rhs(x, m, dims=NN):
    x1, x2, x3 = _split3(x)
    m = m.astype(BF16)
    return _dg(x1, m, dims) + (_dg(x2, m, dims) + _dg(x3, m, dims))


def _sigmoid(x):
    return jax.nn.sigmoid(x)


def _silu(x):
    return x * jax.nn.sigmoid(x)


def _softplus(x):
    return jnp.maximum(x, 0.0) + jnp.log(1.0 + jnp.exp(-jnp.abs(x)))


def _iota2(n, m):
    return (lax.broadcasted_iota(jnp.int32, (n, m), 0), lax.broadcasted_iota(jnp.int32, (n, m), 1))


def _tri_inv(a, ri, ci):
    assert a.shape == (CHUNK, CHUNK) and CHUNK == 4 * INV_BLOCK and INV_BLOCK == 16
    eye = jnp.where(ri == ci, 1.0, 0.0).astype(F32)
    same = lax.shift_right_logical(ri, 4) == lax.shift_right_logical(ci, 4)
    n = jnp.where(same, a, 0.0)
    e = a - n
    n2 = _mm3(n, n)
    n4 = _mm3(n2, n2)
    n8 = _mm3(n4, n4)
    d = _mm3(eye - n, eye + n2)
    d = _mm3(d, eye + n4)
    d = _mm3(d, eye + n8)
    f = _mm3(d, e)
    f2 = _mm3(f, f)
    x = _mm3(eye - f, eye + f2)
    return _mm3(x, d)


def _chunk_masks():
    ri, ci = _iota2(ROWS, ROWS)
    same = lax.shift_right_logical(ri, 6) == lax.shift_right_logical(ci, 6)
    low = jnp.where(same, jnp.where(ci <= ri, 1.0, 0.0), 0.0).astype(BF16)
    up = jnp.where(same, jnp.where(ri <= ci, 1.0, 0.0), 0.0).astype(BF16)
    return low, up


def _norm_proj_kernel(x_ref, g_ref, w_ref, wbt_ref, o_ref, bt_ref, xn_ref):
    @pl.when(pl.program_id(1) == 0)
    def _():
        x = x_ref[...]
        xn = x * lax.rsqrt(jnp.mean(x * x, -1, keepdims=True) + NORM_EPS) * g_ref[...]
        xn_ref[...] = xn.astype(BF16)
        bt_ref[...] = _dg(wbt_ref[...], xn_ref[...], NT)
    o_ref[...] = _dg(xn_ref[...], w_ref[...])


def _norm_proj(x, g, w_packed, w_bt, tm):
    m, d = x.shape
    n = w_packed.shape[1]
    tn = 2048
    return pl.pallas_call(
        _norm_proj_kernel,
        grid=(m // tm, n // tn),
        in_specs=[
            pl.BlockSpec((tm, d), lambda i, j: (i, 0)),
            pl.BlockSpec((1, d), lambda i, j: (0, 0)),
            pl.BlockSpec((d, tn), lambda i, j: (0, j)),
            pl.BlockSpec((8, d), lambda i, j: (0, 0)),
        ],
        out_specs=[
            pl.BlockSpec((tm, tn), lambda i, j: (i, j)),
            pl.BlockSpec((8, tm), lambda i, j: (0, i)),
        ],
        out_shape=[jax.ShapeDtypeStruct((m, n), F32), jax.ShapeDtypeStruct((8, m), F32)],
        scratch_shapes=[pltpu.VMEM((tm, d), BF16)],
        compiler_params=pltpu.CompilerParams(
            dimension_semantics=("arbitrary", "arbitrary"), vmem_limit_bytes=VMEM_LIMIT),
        name="norm_proj",
    )(x, g, w_packed, w_bt)


def _l2norm_rows(x):
    return x * lax.rsqrt(jnp.sum(x * x, -1, keepdims=True) + L2_EPS)


def _dn_out_norm(o, z, nw):
    on = o * lax.rsqrt(jnp.mean(o * o, -1, keepdims=True) + NORM_EPS) * nw
    return on * _silu(z)


def _dn_prompt_kernel(p0_ref, ba_ref, bat_ref, cw_ref, prow_ref, pcol_ref, nw_ref,
                      o_ref, s_out_ref, s_ref, cbuf_ref, *, heads, dk, dv):
    c = pl.program_id(1)
    nqk = heads * dk
    nconv = 2 * nqk + heads * dv

    @pl.when(c == 0)
    def _():
        s_ref[...] = jnp.zeros_like(s_ref)
        cbuf_ref[0:8, :] = jnp.zeros((8, nconv), F32)

    u = p0_ref[:, 0:nconv]
    cbuf_ref[8:8 + ROWS, :] = u
    y = (cw_ref[3:4, :] * u + cw_ref[2:3, :] * cbuf_ref[7:7 + ROWS, :]
         + cw_ref[1:2, :] * cbuf_ref[6:6 + ROWS, :] + cw_ref[0:1, :] * cbuf_ref[5:5 + ROWS, :])
    cbuf_ref[0:8, :] = u[ROWS - 8:ROWS, :]
    qkv = _silu(y)

    ba = ba_ref[...]
    beta_all = _sigmoid(ba)
    g_col = -jnp.exp(prow_ref[0:1, :]) * _softplus(ba + prow_ref[1:2, :])
    bat = bat_ref[...]
    g_row = -jnp.exp(pcol_ref[:, 0:1]) * _softplus(bat + pcol_ref[:, 1:2])
    low, up = _chunk_masks()
    gc_all = _mm_mask_lhs(low, g_col)
    gr_all = _mm_mask_rhs(g_row, up)
    ri, ci = _iota2(CHUNK, CHUNK)
    scale = dk ** -0.5

    for s in range(ROWS // CHUNK):
        r0 = s * CHUNK
        for h in range(heads):
            q = _l2norm_rows(qkv[r0:r0 + CHUNK, h * dk:(h + 1) * dk]) * scale
            k = _l2norm_rows(qkv[r0:r0 + CHUNK, nqk + h * dk:nqk + (h + 1) * dk])
            v = qkv[r0:r0 + CHUNK, 2 * nqk + h * dv:2 * nqk + (h + 1) * dv]
            beta = beta_all[r0:r0 + CHUNK, h:h + 1]
            gc = gc_all[r0:r0 + CHUNK, heads + h:heads + h + 1]
            gr = gr_all[heads + h:heads + h + 1, r0:r0 + CHUNK]
            dec = jnp.where(ci <= ri, jnp.exp(jnp.minimum(gc - gr, 0.0)), 0.0)
            kb = k * beta
            a = jnp.where(ci < ri, _mm3(kb, k, NT) * dec, 0.0)
            t = _tri_inv(a, ri, ci)
            eg = jnp.exp(gc)
            uu = _mm3(t, v * beta)
            ww = _mm3(t, kb * eg)
            st = s_ref[h]
            v_new = uu - _mm3(ww, st)
            qk = jnp.where(ci <= ri, _mm3(q, k, NT) * dec, 0.0)
            o = _mm3(q * eg, st) + _mm3(qk, v_new)
            gl = gc[CHUNK - 1:CHUNK, :]
            s_ref[h] = st * jnp.exp(gl) + _mm3(k * jnp.exp(gl - gc), v_new, TN)
            z = p0_ref[r0:r0 + CHUNK, nconv + h * dv:nconv + (h + 1) * dv]
            o_ref[r0:r0 + CHUNK, h * dv:(h + 1) * dv] = _dn_out_norm(o, z, nw_ref[...]).astype(o_ref.dtype)

    @pl.when(c == pl.num_programs(1) - 1)
    def _():
        s_out_ref[0] = s_ref[...]


def _dn_prompt(p, bat, cw, prow, pcol, nw, batch, nblk, heads, dk, dv):
    m = p.shape[0]
    nconv = 2 * heads * dk + heads * dv
    ba_blk = (2048 + 1792) // LANES
    kern = functools.partial(_dn_prompt_kernel, heads=heads, dk=dk, dv=dv)
    return pl.pallas_call(
        kern,
        grid=(batch, nblk),
        in_specs=[
            pl.BlockSpec((ROWS, 2048), lambda b, c: (b * nblk + c, 0)),
            pl.BlockSpec((ROWS, LANES), lambda b, c: (b * nblk + c, ba_blk)),
            pl.BlockSpec((8, ROWS), lambda b, c: (0, b * nblk + c)),
            pl.BlockSpec((4, nconv), lambda b, c: (0, 0)),
            pl.BlockSpec((2, LANES), lambda b, c: (0, 0)),
            pl.BlockSpec((8, 2), lambda b, c: (0, 0)),
            pl.BlockSpec((1, dv), lambda b, c: (0, 0)),
        ],
        out_specs=[
            pl.BlockSpec((ROWS, heads * dv), lambda b, c: (b * nblk + c, 0)),
            pl.BlockSpec((1, heads, dk, dv), lambda b, c: (b, 0, 0, 0)),
        ],
        out_shape=[jax.ShapeDtypeStruct((m, heads * dv), BF16),
                   jax.ShapeDtypeStruct((batch, heads, dk, dv), F32)],
        scratch_shapes=[pltpu.VMEM((heads, dk, dv), F32), pltpu.VMEM((8 + ROWS, nconv), F32)],
        compiler_params=pltpu.CompilerParams(
            dimension_semantics=("arbitrary", "arbitrary"), vmem_limit_bytes=VMEM_LIMIT),
        name="dn_prompt",
    )(p, p, bat, cw, prow, pcol, nw)


def _col_from_row(row, eye):
    n = row.shape[1]
    return jnp.sum(jnp.where(eye, jnp.broadcast_to(row, (n, n)), 0.0), axis=1, keepdims=True)


def _row_from_col(col, eye):
    n = col.shape[0]
    return jnp.sum(jnp.where(eye, jnp.broadcast_to(col, (n, n)), 0.0), axis=0, keepdims=True)


def _dn_sample_kernel(p0_ref, ba_ref, cache_ref, cw_ref, prow_ref, nw_ref, s_in_ref,
                      o_ref, s_out_ref, q_s, k_s, v_s, g_s, b_s, o_s, *, heads, dk, dv, tb):
    nqk = heads * dk
    nconv = 2 * nqk + heads * dv
    u = p0_ref[:, 0:nconv]
    y = (cw_ref[3:4, :] * u + cw_ref[2:3, :] * cache_ref[2] + cw_ref[1:2, :] * cache_ref[1]
         + cw_ref[0:1, :] * cache_ref[0])
    qkv = _silu(y)
    ba = ba_ref[...]
    b_s[...] = _sigmoid(ba)
    g_s[...] = -jnp.exp(prow_ref[0:1, :]) * _softplus(ba + prow_ref[1:2, :])
    scale = dk ** -0.5
    for h in range(heads):
        q_s[:, h * dk:(h + 1) * dk] = _l2norm_rows(qkv[:, h * dk:(h + 1) * dk]) * scale
        k_s[:, h * dk:(h + 1) * dk] = _l2norm_rows(qkv[:, nqk + h * dk:nqk + (h + 1) * dk])
    v_s[...] = qkv[:, 2 * nqk:]
    ri, ci = _iota2(dk, dk)
    eye = ri == ci

    for i in range(tb):
        for h in range(heads):
            k_row = k_s[i:i + 1, h * dk:(h + 1) * dk]
            q_row = q_s[i:i + 1, h * dk:(h + 1) * dk]
            v_row = v_s[i:i + 1, h * dv:(h + 1) * dv]
            g = g_s[i:i + 1, heads + h:heads + h + 1]
            beta = b_s[i:i + 1, h:h + 1]
            k_col = _col_from_row(k_row, eye)
            q_col = _col_from_row(q_row, eye)
            st = s_in_ref[i, h] * jnp.exp(g)
            kv = jnp.sum(st * k_col, axis=0, keepdims=True)
            delta = (v_row - kv) * beta
            st = st + k_col * delta
            s_out_ref[i, h] = st
            o_s[i:i + 1, h * dv:(h + 1) * dv] = jnp.sum(st * q_col, axis=0, keepdims=True)
    for h in range(heads):
        z = p0_ref[:, nconv + h * dv:nconv + (h + 1) * dv]
        o_ref[:, h * dv:(h + 1) * dv] = _dn_out_norm(
            o_s[:, h * dv:(h + 1) * dv], z, nw_ref[...]).astype(o_ref.dtype)


def _dn_sample(p, cache_t, cw, prow, nw, state, heads, dk, dv, tb=8):
    m = p.shape[0]
    nconv = 2 * heads * dk + heads * dv
    ba_blk = (2048 + 1792) // LANES
    kern = functools.partial(_dn_sample_kernel, heads=heads, dk=dk, dv=dv, tb=tb)
    return pl.pallas_call(
        kern,
        grid=(m // tb,),
        in_specs=[
            pl.BlockSpec((tb, 2048), lambda i: (i, 0)),
            pl.BlockSpec((tb, LANES), lambda i: (i, ba_blk)),
            pl.BlockSpec((3, tb, nconv), lambda i: (0, i, 0)),
            pl.BlockSpec((4, nconv), lambda i: (0, 0)),
            pl.BlockSpec((2, LANES), lambda i: (0, 0)),
            pl.BlockSpec((1, dv), lambda i: (0, 0)),
            pl.BlockSpec((tb, heads, dk, dv), lambda i: (i, 0, 0, 0)),
        ],
        out_specs=[
            pl.BlockSpec((tb, heads * dv), lambda i: (i, 0)),
            pl.BlockSpec((tb, heads, dk, dv), lambda i: (i, 0, 0, 0)),
        ],
        out_shape=[jax.ShapeDtypeStruct((m, heads * dv), BF16),
                   jax.ShapeDtypeStruct(state.shape, F32)],
        scratch_shapes=[pltpu.VMEM((tb, heads * dk), F32), pltpu.VMEM((tb, heads * dk), F32),
                        pltpu.VMEM((tb, heads * dv), F32), pltpu.VMEM((tb, LANES), F32),
                        pltpu.VMEM((tb, LANES), F32), pltpu.VMEM((tb, heads * dv), F32)],
        compiler_params=pltpu.CompilerParams(
            dimension_semantics=("arbitrary",), vmem_limit_bytes=VMEM_LIMIT),
        name="dn_sample",
    )(p, p, cache_t, cw, prow, nw, state)


def _group_sum(x, bd):
    return _mm_mask_rhs(x, bd)


def _rw_prep(prw, prev, par_ref, wwa_ref, g2_ref, bd, nw, rank_w):
    r = prw[:, 0:nw] + (prev[:, 0:nw] - prw[:, 0:nw]) * par_ref[0:1, :]
    kr = prw[:, nw:2 * nw] + (prev[:, nw:2 * nw] - prw[:, nw:2 * nw]) * par_ref[1:2, :]
    vr = prw[:, 2 * nw:3 * nw] + (prev[:, 2 * nw:3 * nw] - prw[:, 2 * nw:3 * nw]) * par_ref[2:3, :]
    c0 = 3 * nw
    lo = prw[:, c0:c0 + 2 * LANES]
    lo = lo + (prev[:, c0:c0 + 2 * LANES] - lo) * par_ref[3:4, 0:2 * LANES]
    wa = lo[:, 0:LANES]
    lane = lax.broadcasted_iota(jnp.int32, wa.shape, 1)
    wa = jnp.where(lane < rank_w, jnp.tanh(wa), wa)
    proj = _mm1(wa, wwa_ref[...])
    w_log = -_softplus(-(par_ref[4:5, :] + proj[:, 0:nw])) - 0.5
    logdec = -jnp.exp(w_log)
    a = _sigmoid(par_ref[5:6, :] + proj[:, nw:2 * nw])
    gate = _mm1(_sigmoid(lo[:, LANES:2 * LANES]), g2_ref[...])
    kk = kr * par_ref[6:7, :]
    kk = kk * lax.rsqrt(_group_sum(kk * kk, bd) + L2_EPS)
    k2 = kr * (1.0 + (a - 1.0) * par_ref[7:8, :])
    return r, logdec, k2, vr, -kk, kk * a, gate


def _rw_epilogue(y, r, k2, v, gate, epi_ref, bd, n):
    mu = _group_sum(y, bd) * (1.0 / n)
    d = y - mu
    var = _group_sum(d * d, bd) * (1.0 / n)
    yn = d * lax.rsqrt(var + GN_EPS) * epi_ref[0:1, :] + epi_ref[1:2, :]
    bonus = _group_sum(r * k2 * epi_ref[2:3, :], bd) * v
    return (yn + bonus) * gate


def _rw_prompt_kernel(p1_ref, par_ref, wwa_ref, g2_ref, bd_ref, epi_ref,
                      o_ref, s_out_ref, s_ref, sbuf_ref, y_s, *, heads, n, rank_w):
    c = pl.program_id(1)
    nw = heads * n
    nrw = 3 * nw + 2 * LANES

    @pl.when(c == 0)
    def _():
        s_ref[...] = jnp.zeros_like(s_ref)
        sbuf_ref[0:8, :] = jnp.zeros((8, nrw), F32)

    prw = p1_ref[:, 0:nrw]
    sbuf_ref[8:8 + ROWS, :] = prw
    prev = sbuf_ref[7:7 + ROWS, :]
    sbuf_ref[0:8, :] = prw[ROWS - 8:ROWS, :]
    bd = bd_ref[...]
    r, logdec, k2, v, a_in, b_in, gate = _rw_prep(prw, prev, par_ref, wwa_ref, g2_ref, bd, nw, rank_w)
    low, _ = _chunk_masks()
    g_all = _mm_mask_lhs(low, logdec)
    ri, ci = _iota2(CHUNK, CHUNK)

    for s in range(ROWS // CHUNK):
        r0 = s * CHUNK
        for h in range(heads):
            sl = (slice(r0, r0 + CHUNK), slice(h * n, (h + 1) * n))
            g = g_all[sl]
            eg = jnp.exp(g)
            ieg = jnp.exp(-g)
            at = a_in[sl] * jnp.exp(g - logdec[sl])
            rt = r[sl] * eg
            bt = b_in[sl] * ieg
            kt = k2[sl] * ieg
            gl = g[CHUNK - 1:CHUNK, :]
            tail = jnp.exp(gl - g)
            vv = v[sl]
            st = s_ref[h]
            aab = jnp.where(ci < ri, _mm3(at, bt, NT), 0.0)
            aak = jnp.where(ci < ri, _mm3(at, kt, NT), 0.0)
            t = _tri_inv(-aab, ri, ci)
            uu = _mm3(t, _mm3(at, st, NT) + _mm3(aak, vv))
            rb = jnp.where(ci <= ri, _mm3(rt, bt, NT), 0.0)
            rk = jnp.where(ci <= ri, _mm3(rt, kt, NT), 0.0)
            y_s[r0:r0 + CHUNK, h * n:(h + 1) * n] = _mm3(rt, st, NT) + _mm3(rb, uu) + _mm3(rk, vv)
            s_ref[h] = (st * jnp.exp(gl) + _mm3(uu, b_in[sl] * tail, TN) + _mm3(vv, k2[sl] * tail, TN))

    o_ref[...] = _rw_epilogue(y_s[...], r, k2, v, gate, epi_ref, bd, n).astype(o_ref.dtype)

    @pl.when(c == pl.num_programs(1) - 1)
    def _():
        s_out_ref[0] = s_ref[...]


def _rw_prompt(p, par, wwa, g2, bd, epi, batch, nblk, heads, n, rank_w):
    m = p.shape[0]
    nw = heads * n
    nrw = 3 * nw + 2 * LANES
    kern = functools.partial(_rw_prompt_kernel, heads=heads, n=n, rank_w=rank_w)
    return pl.pallas_call(
        kern,
        grid=(batch, nblk),
        in_specs=[
            pl.BlockSpec((ROWS, 2048), lambda b, c: (b * nblk + c, 1)),
            pl.BlockSpec((8, nw), lambda b, c: (0, 0)),
            pl.BlockSpec((LANES, 2 * nw), lambda b, c: (0, 0)),
            pl.BlockSpec((LANES, nw), lambda b, c: (0, 0)),
            pl.BlockSpec((nw, nw), lambda b, c: (0, 0)),
            pl.BlockSpec((8, nw), lambda b, c: (0, 0)),
        ],
        out_specs=[
            pl.BlockSpec((ROWS, nw), lambda b, c: (b * nblk + c, 0)),
            pl.BlockSpec((1, heads, n, n), lambda b, c: (b, 0, 0, 0)),
        ],
        out_shape=[jax.ShapeDtypeStruct((m, nw), BF16),
                   jax.ShapeDtypeStruct((batch, heads, n, n), F32)],
        scratch_shapes=[pltpu.VMEM((heads, n, n), F32), pltpu.VMEM((8 + ROWS, nrw), F32),
                        pltpu.VMEM((ROWS, nw), F32)],
        compiler_params=pltpu.CompilerParams(
            dimension_semantics=("arbitrary", "arbitrary"), vmem_limit_bytes=VMEM_LIMIT),
        name="rw_prompt",
    )(p, par, wwa, g2, bd, epi)


def _rw_sample_kernel(p1_ref, prev_ref, par_ref, wwa_ref, g2_ref, bd_ref, epi_ref, s_in_ref,
                      o_ref, s_out_ref, r_s, w_s, k_s, v_s, a_s, b_s, y_s, *, heads, n, rank_w, tb):
    nw = heads * n
    nrw = 3 * nw + 2 * LANES
    bd = bd_ref[...]
    prw = p1_ref[:, 0:nrw]
    r, logdec, k2, v, a_in, b_in, gate = _rw_prep(prw, prev_ref[...], par_ref, wwa_ref, g2_ref, bd, nw, rank_w)
    r_s[...] = r
    w_s[...] = jnp.exp(logdec)
    k_s[...] = k2
    v_s[...] = v
    a_s[...] = a_in
    b_s[...] = b_in
    ri, ci = _iota2(n, n)
    eye = ri == ci

    for i in range(tb):
        for h in range(heads):
            sl = (slice(i, i + 1), slice(h * n, (h + 1) * n))
            st = s_in_ref[i, h]
            sa = jnp.sum(st * a_s[sl], axis=1, keepdims=True)
            v_col = _col_from_row(v_s[sl], eye)
            st = st * w_s[sl] + sa * b_s[sl] + v_col * k_s[sl]
            s_out_ref[i, h] = st
            y_col = jnp.sum(st * r_s[sl], axis=1, keepdims=True)
            y_s[sl] = _row_from_col(y_col, eye)
    o_ref[...] = _rw_epilogue(y_s[...], r, k2, v, gate, epi_ref, bd, n).astype(o_ref.dtype)


def _rw_sample(p, prev, par, wwa, g2, bd, epi, state, heads, n, rank_w, tb=8):
    m = p.shape[0]
    nw = heads * n
    nrw = 3 * nw + 2 * LANES
    kern = functools.partial(_rw_sample_kernel, heads=heads, n=n, rank_w=rank_w, tb=tb)
    return pl.pallas_call(
        kern,
        grid=(m // tb,),
        in_specs=[
            pl.BlockSpec((tb, 2048), lambda i: (i, 1)),
            pl.BlockSpec((tb, nrw), lambda i: (i, 0)),
            pl.BlockSpec((8, nw), lambda i: (0, 0)),
            pl.BlockSpec((LANES, 2 * nw), lambda i: (0, 0)),
            pl.BlockSpec((LANES, nw), lambda i: (0, 0)),
            pl.BlockSpec((nw, nw), lambda i: (0, 0)),
            pl.BlockSpec((8, nw), lambda i: (0, 0)),
            pl.BlockSpec((tb, heads, n, n), lambda i: (i, 0, 0, 0)),
        ],
        out_specs=[
            pl.BlockSpec((tb, nw), lambda i: (i, 0)),
            pl.BlockSpec((tb, heads, n, n), lambda i: (i, 0, 0, 0)),
        ],
        out_shape=[jax.ShapeDtypeStruct((m, nw), BF16), jax.ShapeDtypeStruct(state.shape, F32)],
        scratch_shapes=[pltpu.VMEM((tb, nw), F32)] * 7,
        compiler_params=pltpu.CompilerParams(
            dimension_semantics=("arbitrary",), vmem_limit_bytes=VMEM_LIMIT),
        name="rw_sample",
    )(p, prev, par, wwa, g2, bd, epi, state)


def _merge_kernel(x_ref, odn_ref, orw_ref, gt_ref, wd_ref, wr_ref, wo_ref, h_ref, *, d):
    ga = _sigmoid(gt_ref[:, 0:d])
    gb = _sigmoid(gt_ref[:, d:2 * d])
    m = ga * _dg(odn_ref[...], wd_ref[...]) + gb * _dg(orw_ref[...], wr_ref[...])
    h_ref[...] = x_ref[...] + _dg(m.astype(BF16), wo_ref[...])


def _merge(x, odn, orw, p, wd, wr, wo, batch, nblk_in, nblk_out, skip, tm):
    d = x.shape[1]
    nd = odn.shape[1]
    nr = orw.shape[1]
    kern = functools.partial(_merge_kernel, d=d)
    in_row = lambda b, i: (b * nblk_in + skip + i, 0)
    return pl.pallas_call(
        kern,
        grid=(batch, nblk_out),
        in_specs=[
            pl.BlockSpec((tm, d), in_row),
            pl.BlockSpec((tm, nd), in_row),
            pl.BlockSpec((tm, nr), in_row),
            pl.BlockSpec((tm, 2 * d), lambda b, i: (b * nblk_in + skip + i, 2)),
            pl.BlockSpec((nd, d), lambda b, i: (0, 0)),
            pl.BlockSpec((nr, d), lambda b, i: (0, 0)),
            pl.BlockSpec((d, d), lambda b, i: (0, 0)),
        ],
        out_specs=pl.BlockSpec((tm, d), lambda b, i: (b * nblk_out + i, 0)),
        out_shape=jax.ShapeDtypeStruct((batch * nblk_out * tm, d), F32),
        compiler_params=pltpu.CompilerParams(
            dimension_semantics=("arbitrary", "arbitrary"), vmem_limit_bytes=VMEM_LIMIT),
        name="merge",
    )(x, odn, orw, p, wd, wr, wo)


def _top_values(x, k):
    out = []
    cur = x
    for _ in range(k):
        m = jnp.max(cur, axis=0, keepdims=True)
        out.append(m)
        cur = jnp.where(cur == m, -jnp.inf, cur)
    return out


def _peer_kernel(h_ref, ln2_ref, wq_ref, keys_ref, u_ref, vt_ref, lnf_ref, y_ref,
                 xn_s, q_s, sc_s, top_s, e0_s, e1_s, th_s, ht_s, g_s, acc_s, *, heads, nkeys, eb):
    e = pl.program_id(1)
    ne = pl.num_programs(1)
    k = PEER_TOPK

    @pl.when(e == 0)
    def _():
        x = h_ref[...]
        xn = x * lax.rsqrt(jnp.mean(x * x, -1, keepdims=True) + NORM_EPS) * ln2_ref[...]
        xn_s[...] = xn.astype(BF16)
        q = _dg(xn_s[...], wq_ref[...]).astype(BF16)
        for i in range(2 * heads):
            q_s[i] = q[:, i * nkeys:(i + 1) * nkeys]

        def score_body(i, carry):
            s = _dg(keys_ref[i], q_s[i], NT)
            sc_s[i] = s
            top_s[i] = jnp.concatenate(_top_values(s, k), axis=0)
            return carry

        lax.fori_loop(0, 2 * heads, score_body, 0)

        def head_body(hd, carry):
            a = top_s[2 * hd]
            b = top_s[2 * hd + 1]
            cand = jnp.concatenate([a[i:i + 1, :] + b for i in range(k)], axis=0)
            cv = _top_values(cand, k)
            z = jnp.exp(cv[0] - cv[0])
            for j in range(1, k):
                z = z + jnp.exp(cv[j] - cv[0])
            e0_s[hd] = jnp.exp(sc_s[2 * hd] - a[0:1, :])
            e1_s[hd] = jnp.exp(sc_s[2 * hd + 1] - b[0:1, :]) / z
            th_s[hd] = jnp.broadcast_to(cv[k - 1], th_s.shape[1:])
            return carry

        lax.fori_loop(0, heads, head_body, 0)
        acc_s[...] = jnp.zeros_like(acc_s)

    ht_s[...] = _dg(u_ref[...], xn_s[...], NT)
    per = eb // nkeys
    assert per % 8 == 0
    i0 = pl.multiple_of(e * per, 8)
    for il in range(per):
        w = jnp.zeros((nkeys, ht_s.shape[1]), F32)
        for hd in range(heads):
            s0 = sc_s[2 * hd, pl.ds(i0, per), :]
            e0 = e0_s[hd, pl.ds(i0, per), :]
            cand = sc_s[2 * hd + 1] + s0[il:il + 1, :]
            w = w + jnp.where(cand >= th_s[hd, 0:1, :], e1_s[hd], 0.0) * e0[il:il + 1, :]
        g = jax.nn.gelu(ht_s[il * nkeys:(il + 1) * nkeys, :]) * w
        g_s[il * nkeys:(il + 1) * nkeys, :] = g.astype(BF16)
    acc_s[...] += _dg(vt_ref[...], g_s[...])

    @pl.when(e == ne - 1)
    def _():
        out = h_ref[...] + acc_s[...].T
        y_ref[...] = out * lax.rsqrt(jnp.mean(out * out, -1, keepdims=True) + NORM_EPS) * lnf_ref[...]


def _peer(h, ln2, wq, keys, u, vt, lnf, tb, eb=1024):
    m, d = h.shape
    nq = wq.shape[1]
    nk2, nkeys, half = keys.shape
    heads = nk2 // 2
    nexp = u.shape[0]
    kern = functools.partial(_peer_kernel, heads=heads, nkeys=nkeys, eb=eb)
    return pl.pallas_call(
        kern,
        grid=(m // tb, nexp // eb),
        in_specs=[
            pl.BlockSpec((tb, d), lambda t, e: (t, 0)),
            pl.BlockSpec((1, d), lambda t, e: (0, 0)),
            pl.BlockSpec((d, nq), lambda t, e: (0, 0)),
            pl.BlockSpec((nk2, nkeys, half), lambda t, e: (0, 0, 0)),
            pl.BlockSpec((eb, d), lambda t, e: (e, 0)),
            pl.BlockSpec((d, eb), lambda t, e: (0, e)),
            pl.BlockSpec((1, d), lambda t, e: (0, 0)),
        ],
        out_specs=pl.BlockSpec((tb, d), lambda t, e: (t, 0)),
        out_shape=jax.ShapeDtypeStruct((m, d), F32),
        scratch_shapes=[
            pltpu.VMEM((tb, d), BF16),
            pltpu.VMEM((nk2, tb, half), BF16),
            pltpu.VMEM((nk2, nkeys, tb), F32),
            pltpu.VMEM((nk2, PEER_TOPK, tb), F32),
            pltpu.VMEM((heads, nkeys, tb), F32),
            pltpu.VMEM((heads, nkeys, tb), F32),
            pltpu.VMEM((heads, 8, tb), F32),
            pltpu.VMEM((eb, tb), F32),
            pltpu.VMEM((eb, tb), BF16),
            pltpu.VMEM((d, tb), F32),
        ],
        compiler_params=pltpu.CompilerParams(
            dimension_semantics=("arbitrary", "arbitrary"), vmem_limit_bytes=VMEM_LIMIT),
        name="peer",
    )(h, ln2, wq, keys, u, vt, lnf)


def _pad_lanes(v, width, offset=0):
    out = jnp.zeros((width,), F32)
    return out.at[offset:offset + v.shape[0]].set(v.astype(F32))


def kernel(x_prompt, x_sample, cache_dn_conv, state_dn, cache_rw_shift, state_rw, meta_tokens, ln1, w_in, dn_conv_w, dn_a_log, dn_dt_bias, dn_norm_w, rw_mu, rw_w0, rw_w2, rw_a0, rw_a2, rw_g2, rw_k_k, rw_k_a, rw_r_k, rw_gn_w, rw_gn_b, w_up_dn, w_up_rw, w_out, ln2, peer_wq, peer_keys, peer_u, peer_v, ln_f):
    bp, seq, d = x_prompt.shape
    bs, seq_s, _ = x_sample.shape
    depth = w_in.shape[0]
    assert depth == 1 and seq_s == 1
    n_meta = meta_tokens.shape[0]
    _, _, heads_dn, dk, dv = state_dn.shape
    _, _, heads_rw, n_rw, _ = state_rw.shape
    n_conv = dn_conv_w.shape[2]
    nqkv = 2 * heads_dn * dk + heads_dn * dv
    assert n_conv == nqkv == 1536 and dn_conv_w.shape[1] == 4 and heads_dn * dv == 512
    nw = heads_rw * n_rw
    rank_w, rank_a, rank_g = rw_w2.shape[1], rw_a2.shape[1], rw_g2.shape[1]
    assert nw == 512 and rank_w + rank_a == LANES and rank_g == LANES
    n_rwp = 3 * nw + rank_w + rank_a + rank_g
    o_z = nqkv
    o_beta = o_z + heads_dn * dv
    o_alpha = o_beta + heads_dn
    o_rw = o_alpha + heads_dn
    o_gate = o_rw + n_rwp
    assert w_in.shape[2] == o_gate + 2 * d and 2 * heads_dn == 8

    wi = w_in[0]
    seg1 = jnp.concatenate([wi[:, o_rw:o_gate], wi[:, o_beta:o_rw],
                            jnp.zeros((d, 2048 - n_rwp - 2 * heads_dn), F32)], axis=1)
    w_packed = jnp.concatenate([wi[:, 0:o_beta], seg1, wi[:, o_gate:]], axis=1).astype(BF16)
    w_bt = wi[:, o_beta:o_rw].T.astype(BF16)
    g1 = ln1[0][None, :]
    cw = dn_conv_w[0]
    prow = jnp.stack([_pad_lanes(dn_a_log[0], LANES, heads_dn), _pad_lanes(dn_dt_bias[0], LANES, heads_dn)])
    pcol = jnp.stack([_pad_lanes(dn_a_log[0], 8, heads_dn), _pad_lanes(dn_dt_bias[0], 8, heads_dn)], axis=1)
    nwd = dn_norm_w[0][None, :]
    mu = rw_mu[0]
    par = jnp.stack([mu[0:nw], mu[nw:2 * nw], mu[2 * nw:3 * nw], _pad_lanes(mu[3 * nw:], nw),
                     rw_w0[0], rw_a0[0], rw_k_k[0], rw_k_a[0]])
    wwa = jnp.zeros((LANES, 2 * nw), F32)
    wwa = wwa.at[0:rank_w, 0:nw].set(rw_w2[0]).at[rank_w:, nw:].set(rw_a2[0]).astype(BF16)
    g2 = rw_g2[0].astype(BF16)
    grp = jnp.arange(nw) // n_rw
    bd = (grp[:, None] == grp[None, :]).astype(BF16)
    epi = jnp.zeros((8, nw), F32).at[0].set(rw_gn_w[0]).at[1].set(rw_gn_b[0]).at[2].set(rw_r_k[0].reshape(-1))
    wd = w_up_dn[0].astype(BF16)
    wr = w_up_rw[0].astype(BF16)
    wo = w_out[0].astype(BF16)
    l2 = ln2[0][None, :]
    wq = peer_wq[0].astype(BF16)
    pk = peer_keys[0]
    keys = pk.reshape(pk.shape[0] * pk.shape[1], pk.shape[2], pk.shape[3]).astype(BF16)
    pu = peer_u[0].astype(BF16)
    pvt = peer_v[0].T.astype(BF16)
    lf = ln_f[None, :]

    t_real = n_meta + seq
    nblk = -(-t_real // ROWS)
    t_pad = nblk * ROWS
    n_front = t_pad - t_real
    assert (n_front + n_meta) % ROWS == 0 and seq % ROWS == 0
    skip = (n_front + n_meta) // ROWS
    xp = jnp.concatenate([jnp.zeros((bp, n_front, d), F32),
                          jnp.broadcast_to(meta_tokens.astype(F32), (bp, n_meta, d)), x_prompt], axis=1)
    xp = xp.reshape(bp * t_pad, d)
    tm = max(t for t in (1024, 512, 256, ROWS) if (bp * t_pad) % t == 0)
    pp, batp = _norm_proj(xp, g1, w_packed, w_bt, tm)
    odn_p, p_dn = _dn_prompt(pp, batp, cw, prow, pcol, nwd, bp, nblk, heads_dn, dk, dv)
    orw_p, p_rw = _rw_prompt(pp, par, wwa, g2, bd, epi, bp, nblk, heads_rw, n_rw, rank_w)
    hp = _merge(xp, odn_p, orw_p, pp, wd, wr, wo, bp, nblk, seq // ROWS, skip, ROWS)
    y_prompt = _peer(hp, l2, wq, keys, pu, pvt, lf, 512).reshape(bp, seq, d)
    pp3 = pp.reshape(bp, t_pad, -1)
    p_conv = pp3[:, t_pad - 3:, 0:nqkv]
    p_shift = pp3[:, t_pad - 1:, 2048:2048 + n_rwp]

    xs = x_sample.reshape(bs, d)
    ps, _ = _norm_proj(xs, g1, w_packed, w_bt, bs)
    cache_t = jnp.moveaxis(cache_dn_conv[0], 1, 0)
    odn_s, s_dn = _dn_sample(ps, cache_t, cw, prow, nwd, state_dn[0], heads_dn, dk, dv)
    orw_s, s_rw = _rw_sample(ps, cache_rw_shift[0][:, 0, :], par, wwa, g2, bd, epi, state_rw[0],
                             heads_rw, n_rw, rank_w)
    hs = _merge(xs, odn_s, orw_s, ps, wd, wr, wo, 1, 1, 1, 0, bs)
    y_sample = _peer(hs, l2, wq, keys, pu, pvt, lf, bs).reshape(bs, 1, d)
    s_conv = jnp.concatenate([cache_dn_conv[0][:, 1:], ps[:, None, 0:nqkv]], axis=1)
    s_shift = ps[:, None, 2048:2048 + n_rwp]

    return (y_prompt, y_sample, p_conv[None], p_dn[None], p_shift[None], p_rw[None],
            s_conv[None], s_dn[None], s_shift[None], s_rw[None])
```

```python
import functools

import jax
import jax.numpy as jnp
from jax import lax
from jax.experimental import pallas as pl
from jax.experimental.pallas import tpu as pltpu

F32 = jnp.float32
BF16 = jnp.bfloat16

NORM_EPS = 1e-6
L2_EPS = 1e-6
GN_EPS = 64e-5
PEER_TOPK = 16

LANES = 128
CHUNK = 64
ROWS = 2 * CHUNK
INV_BLOCK = 16
VMEM_LIMIT = 52 * 1024 * 1024

NN = ((1,), (0,))
NT = ((1,), (1,))
TN = ((0,), (0,))


def _dg(a, b, dims=NN):
    return lax.dot_general(a, b, (dims, ((), ())), preferred_element_type=F32)


def _mm1(a, b, dims=NN):
    return _dg(a.astype(BF16), b.astype(BF16), dims)


def _hi_lo(x):
    hi = x.astype(BF16)
    lo = (x - hi.astype(F32)).astype(BF16)
    return hi, lo


def _mm3(a, b, dims=NN):
    a1, a2 = _hi_lo(a)
    b1, b2 = _hi_lo(b)
    return _dg(a1, b1, dims) + (_dg(a1, b2, dims) + _dg(a2, b1, dims))


def _split3(x):
    x1 = x.astype(BF16)
    r1 = x - x1.astype(F32)
    x2 = r1.astype(BF16)
    x3 = (r1 - x2.astype(F32)).astype(BF16)
    return x1, x2, x3


def _mm_mask_lhs(m, x, dims=NN):
    x1, x2, x3 = _split3(x)
    m = m.astype(BF16)
    return _dg(m, x1, dims) + (_dg(m, x2, dims) + _dg(m, x3, dims))


def _mm_mask_rhs(x, m, dims=NN):
    x1, x2, x3 = _split3(x)
    m = m.astype(BF16)
    return _dg(x1, m, dims) + (_dg(x2, m, dims) + _dg(x3, m, dims))


def _sigmoid(x):
    return jax.nn.sigmoid(x)


def _silu(x):
    return x * jax.nn.sigmoid(x)


def _softplus(x):
    return jnp.maximum(x, 0.0) + jnp.log(1.0 + jnp.exp(-jnp.abs(x)))


def _iota2(n, m):
    return (lax.broadcasted_iota(jnp.int32, (n, m), 0), lax.broadcasted_iota(jnp.int32, (n, m), 1))


def _mm3_each(pairs, dims=NN):
    split = [(_hi_lo(a), _hi_lo(b)) for a, b in pairs]
    return [_dg(a1, b1, dims) + (_dg(a1, b2, dims) + _dg(a2, b1, dims)) for (a1, a2), (b1, b2) in split]


def _tri_inv_each(mats, ri, ci):
    assert mats[0].shape == (CHUNK, CHUNK) and CHUNK == 4 * INV_BLOCK and INV_BLOCK == 16
    eye = jnp.where(ri == ci, 1.0, 0.0).astype(F32)
    same = lax.shift_right_logical(ri, 4) == lax.shift_right_logical(ci, 4)
    n = [jnp.where(same, a, 0.0) for a in mats]
    e = [a - x for a, x in zip(mats, n)]
    n2 = _mm3_each(list(zip(n, n)))
    n4 = _mm3_each(list(zip(n2, n2)))
    n8 = _mm3_each(list(zip(n4, n4)))
    d = _mm3_each([(eye - x, eye + y) for x, y in zip(n, n2)])
    d = _mm3_each([(x, eye + y) for x, y in zip(d, n4)])
    d = _mm3_each([(x, eye + y) for x, y in zip(d, n8)])
    f = _mm3_each(list(zip(d, e)))
    f2 = _mm3_each(list(zip(f, f)))
    x = _mm3_each([(eye - p, eye + q) for p, q in zip(f, f2)])
    return _mm3_each(list(zip(x, d)))


def _chunk_masks():
    ri, ci = _iota2(ROWS, ROWS)
    same = lax.shift_right_logical(ri, 6) == lax.shift_right_logical(ci, 6)
    low = jnp.where(same, jnp.where(ci <= ri, 1.0, 0.0), 0.0).astype(BF16)
    up = jnp.where(same, jnp.where(ri <= ci, 1.0, 0.0), 0.0).astype(BF16)
    return low, up


def _norm_proj_kernel(x_ref, g_ref, w_ref, wbt_ref, o_ref, bt_ref, xn_ref):
    @pl.when(pl.program_id(1) == 0)
    def _():
        x = x_ref[...]
        xn = x * lax.rsqrt(jnp.mean(x * x, -1, keepdims=True) + NORM_EPS) * g_ref[...]
        xn_ref[...] = xn.astype(BF16)
        bt_ref[...] = _dg(wbt_ref[...], xn_ref[...], NT)
    o_ref[...] = _dg(xn_ref[...], w_ref[...])


def _norm_proj(x, g, w_packed, w_bt, tm):
    m, d = x.shape
    n = w_packed.shape[1]
    tn = 2048
    return pl.pallas_call(
        _norm_proj_kernel,
        grid=(m // tm, n // tn),
        in_specs=[
            pl.BlockSpec((tm, d), lambda i, j: (i, 0)),
            pl.BlockSpec((1, d), lambda i, j: (0, 0)),
            pl.BlockSpec((d, tn), lambda i, j: (0, j)),
            pl.BlockSpec((8, d), lambda i, j: (0, 0)),
        ],
        out_specs=[
            pl.BlockSpec((tm, tn), lambda i, j: (i, j)),
            pl.BlockSpec((8, tm), lambda i, j: (0, i)),
        ],
        out_shape=[jax.ShapeDtypeStruct((m, n), F32), jax.ShapeDtypeStruct((8, m), F32)],
        scratch_shapes=[pltpu.VMEM((tm, d), BF16)],
        compiler_params=pltpu.CompilerParams(
            dimension_semantics=("arbitrary", "arbitrary"), vmem_limit_bytes=VMEM_LIMIT),
        name="norm_proj",
    )(x, g, w_packed, w_bt)


def _l2norm_rows(x):
    return x * lax.rsqrt(jnp.sum(x * x, -1, keepdims=True) + L2_EPS)


def _dn_out_norm(o, z, nw):
    on = o * lax.rsqrt(jnp.mean(o * o, -1, keepdims=True) + NORM_EPS) * nw
    return on * _silu(z)


def _dn_prompt_kernel(p0_ref, ba_ref, bat_ref, cw_ref, prow_ref, pcol_ref, nw_ref,
                      o_ref, s_out_ref, s_ref, cbuf_ref, *, heads, dk, dv):
    c = pl.program_id(1)
    nqk = heads * dk
    nconv = 2 * nqk + heads * dv

    @pl.when(c == 0)
    def _():
        s_ref[...] = jnp.zeros_like(s_ref)
        cbuf_ref[0:8, :] = jnp.zeros((8, nconv), F32)

    u = p0_ref[:, 0:nconv]
    cbuf_ref[8:8 + ROWS, :] = u
    y = (cw_ref[3:4, :] * u + cw_ref[2:3, :] * cbuf_ref[7:7 + ROWS, :]
         + cw_ref[1:2, :] * cbuf_ref[6:6 + ROWS, :] + cw_ref[0:1, :] * cbuf_ref[5:5 + ROWS, :])
    cbuf_ref[0:8, :] = u[ROWS - 8:ROWS, :]
    qkv = _silu(y)

    ba = ba_ref[...]
    beta_all = _sigmoid(ba)
    g_col = -jnp.exp(prow_ref[0:1, :]) * _softplus(ba + prow_ref[1:2, :])
    bat = bat_ref[...]
    g_row = -jnp.exp(pcol_ref[:, 0:1]) * _softplus(bat + pcol_ref[:, 1:2])
    low, up = _chunk_masks()
    gc_all = _mm_mask_lhs(low, g_col)
    gr_all = _mm_mask_rhs(g_row, up)
    ri, ci = _iota2(CHUNK, CHUNK)
    scale = dk ** -0.5

    hcs = [(s, h) for s in range(ROWS // CHUNK) for h in range(heads)]
    q_l, k_l, kb_l, vb_l, gc_l, dec_l = [], [], [], [], [], []
    for s, h in hcs:
        r0 = s * CHUNK
        q_l.append(_l2norm_rows(qkv[r0:r0 + CHUNK, h * dk:(h + 1) * dk]) * scale)
        k = _l2norm_rows(qkv[r0:r0 + CHUNK, nqk + h * dk:nqk + (h + 1) * dk])
        beta = beta_all[r0:r0 + CHUNK, h:h + 1]
        gc = gc_all[r0:r0 + CHUNK, heads + h:heads + h + 1]
        gr = gr_all[heads + h:heads + h + 1, r0:r0 + CHUNK]
        k_l.append(k)
        kb_l.append(k * beta)
        vb_l.append(qkv[r0:r0 + CHUNK, 2 * nqk + h * dv:2 * nqk + (h + 1) * dv] * beta)
        gc_l.append(gc)
        dec_l.append(jnp.where(ci <= ri, jnp.exp(jnp.minimum(gc - gr, 0.0)), 0.0))
    a_l = [jnp.where(ci < ri, x * dec, 0.0)
           for x, dec in zip(_mm3_each(list(zip(kb_l, k_l)), NT), dec_l)]
    t_l = _tri_inv_each(a_l, ri, ci)
    eg_l = [jnp.exp(gc) for gc in gc_l]
    uu_l = _mm3_each(list(zip(t_l, vb_l)))
    ww_l = _mm3_each([(t, kb * eg) for t, kb, eg in zip(t_l, kb_l, eg_l)])
    qk_l = [jnp.where(ci <= ri, x * dec, 0.0)
            for x, dec in zip(_mm3_each(list(zip(q_l, k_l)), NT), dec_l)]

    for s in range(ROWS // CHUNK):
        r0 = s * CHUNK
        ix = [s * heads + h for h in range(heads)]
        st_l = [s_ref[h] for h in range(heads)]
        ws_l = _mm3_each([(ww_l[i], st) for i, st in zip(ix, st_l)])
        qs_l = _mm3_each([(q_l[i] * eg_l[i], st) for i, st in zip(ix, st_l)])
        vn_l = [uu_l[i] - ws for i, ws in zip(ix, ws_l)]
        qv_l = _mm3_each([(qk_l[i], vn) for i, vn in zip(ix, vn_l)])
        gl_l = [gc_l[i][CHUNK - 1:CHUNK, :] for i in ix]
        kv_l = _mm3_each([(k_l[i] * jnp.exp(gl - gc_l[i]), vn) for i, gl, vn in zip(ix, gl_l, vn_l)], TN)
        for h in range(heads):
            s_ref[h] = st_l[h] * jnp.exp(gl_l[h]) + kv_l[h]
            z = p0_ref[r0:r0 + CHUNK, nconv + h * dv:nconv + (h + 1) * dv]
            o_ref[r0:r0 + CHUNK, h * dv:(h + 1) * dv] = _dn_out_norm(
                qs_l[h] + qv_l[h], z, nw_ref[...]).astype(o_ref.dtype)

    @pl.when(c == pl.num_programs(1) - 1)
    def _():
        s_out_ref[0] = s_ref[...]


def _dn_prompt(p, bat, cw, prow, pcol, nw, batch, nblk, heads, dk, dv):
    m = p.shape[0]
    nconv = 2 * heads * dk + heads * dv
    ba_blk = (2048 + 1792) // LANES
    kern = functools.partial(_dn_prompt_kernel, heads=heads, dk=dk, dv=dv)
    return pl.pallas_call(
        kern,
        grid=(batch, nblk),
        in_specs=[
            pl.BlockSpec((ROWS, 2048), lambda b, c: (b * nblk + c, 0)),
            pl.BlockSpec((ROWS, LANES), lambda b, c: (b * nblk + c, ba_blk)),
            pl.BlockSpec((8, ROWS), lambda b, c: (0, b * nblk + c)),
            pl.BlockSpec((4, nconv), lambda b, c: (0, 0)),
            pl.BlockSpec((2, LANES), lambda b, c: (0, 0)),
            pl.BlockSpec((8, 2), lambda b, c: (0, 0)),
            pl.BlockSpec((1, dv), lambda b, c: (0, 0)),
        ],
        out_specs=[
            pl.BlockSpec((ROWS, heads * dv), lambda b, c: (b * nblk + c, 0)),
            pl.BlockSpec((1, heads, dk, dv), lambda b, c: (b, 0, 0, 0)),
        ],
        out_shape=[jax.ShapeDtypeStruct((m, heads * dv), BF16),
                   jax.ShapeDtypeStruct((batch, heads, dk, dv), F32)],
        scratch_shapes=[pltpu.VMEM((heads, dk, dv), F32), pltpu.VMEM((8 + ROWS, nconv), F32)],
        compiler_params=pltpu.CompilerParams(
            dimension_semantics=("arbitrary", "arbitrary"), vmem_limit_bytes=VMEM_LIMIT),
        name="dn_prompt",
    )(p, p, bat, cw, prow, pcol, nw)


def _col_from_row(row, eye):
    n = row.shape[1]
    return jnp.sum(jnp.where(eye, jnp.broadcast_to(row, (n, n)), 0.0), axis=1, keepdims=True)


def _row_from_col(col, eye):
    n = col.shape[0]
    return jnp.sum(jnp.where(eye, jnp.broadcast_to(col, (n, n)), 0.0), axis=0, keepdims=True)


def _dn_sample_kernel(p0_ref, ba_ref, cache_ref, cw_ref, prow_ref, nw_ref, s_in_ref,
                      o_ref, s_out_ref, q_s, k_s, v_s, g_s, b_s, o_s, *, heads, dk, dv, tb):
    nqk = heads * dk
    nconv = 2 * nqk + heads * dv
    u = p0_ref[:, 0:nconv]
    y = (cw_ref[3:4, :] * u + cw_ref[2:3, :] * cache_ref[2] + cw_ref[1:2, :] * cache_ref[1]
         + cw_ref[0:1, :] * cache_ref[0])
    qkv = _silu(y)
    ba = ba_ref[...]
    b_s[...] = _sigmoid(ba)
    g_s[...] = -jnp.exp(prow_ref[0:1, :]) * _softplus(ba + prow_ref[1:2, :])
    scale = dk ** -0.5
    for h in range(heads):
        q_s[:, h * dk:(h + 1) * dk] = _l2norm_rows(qkv[:, h * dk:(h + 1) * dk]) * scale
        k_s[:, h * dk:(h + 1) * dk] = _l2norm_rows(qkv[:, nqk + h * dk:nqk + (h + 1) * dk])
    v_s[...] = qkv[:, 2 * nqk:]
    ri, ci = _iota2(dk, dk)
    eye = ri == ci

    for i in range(tb):
        for h in range(heads):
            k_row = k_s[i:i + 1, h * dk:(h + 1) * dk]
            q_row = q_s[i:i + 1, h * dk:(h + 1) * dk]
            v_row = v_s[i:i + 1, h * dv:(h + 1) * dv]
            g = g_s[i:i + 1, heads + h:heads + h + 1]
            beta = b_s[i:i + 1, h:h + 1]
            k_col = _col_from_row(k_row, eye)
            q_col = _col_from_row(q_row, eye)
            st = s_in_ref[i, h] * jnp.exp(g)
            kv = jnp.sum(st * k_col, axis=0, keepdims=True)
            delta = (v_row - kv) * beta
            st = st + k_col * delta
            s_out_ref[i, h] = st
            o_s[i:i + 1, h * dv:(h + 1) * dv] = jnp.sum(st * q_col, axis=0, keepdims=True)
    for h in range(heads):
        z = p0_ref[:, nconv + h * dv:nconv + (h + 1) * dv]
        o_ref[:, h * dv:(h + 1) * dv] = _dn_out_norm(
            o_s[:, h * dv:(h + 1) * dv], z, nw_ref[...]).astype(o_ref.dtype)


def _dn_sample(p, cache_t, cw, prow, nw, state, heads, dk, dv, tb=8):
    m = p.shape[0]
    nconv = 2 * heads * dk + heads * dv
    ba_blk = (2048 + 1792) // LANES
    kern = functools.partial(_dn_sample_kernel, heads=heads, dk=dk, dv=dv, tb=tb)
    return pl.pallas_call(
        kern,
        grid=(m // tb,),
        in_specs=[
            pl.BlockSpec((tb, 2048), lambda i: (i, 0)),
            pl.BlockSpec((tb, LANES), lambda i: (i, ba_blk)),
            pl.BlockSpec((3, tb, nconv), lambda i: (0, i, 0)),
            pl.BlockSpec((4, nconv), lambda i: (0, 0)),
            pl.BlockSpec((2, LANES), lambda i: (0, 0)),
            pl.BlockSpec((1, dv), lambda i: (0, 0)),
            pl.BlockSpec((tb, heads, dk, dv), lambda i: (i, 0, 0, 0)),
        ],
        out_specs=[
            pl.BlockSpec((tb, heads * dv), lambda i: (i, 0)),
            pl.BlockSpec((tb, heads, dk, dv), lambda i: (i, 0, 0, 0)),
        ],
        out_shape=[jax.ShapeDtypeStruct((m, heads * dv), BF16),
                   jax.ShapeDtypeStruct(state.shape, F32)],
        scratch_shapes=[pltpu.VMEM((tb, heads * dk), F32), pltpu.VMEM((tb, heads * dk), F32),
                        pltpu.VMEM((tb, heads * dv), F32), pltpu.VMEM((tb, LANES), F32),
                        pltpu.VMEM((tb, LANES), F32), pltpu.VMEM((tb, heads * dv), F32)],
        compiler_params=pltpu.CompilerParams(
            dimension_semantics=("arbitrary",), vmem_limit_bytes=VMEM_LIMIT),
        name="dn_sample",
    )(p, p, cache_t, cw, prow, nw, state)


def _group_sum(x, bd):
    return _mm_mask_rhs(x, bd)


def _rw_prep(prw, prev, par_ref, wwa_ref, g2_ref, bd, nw, rank_w):
    r = prw[:, 0:nw] + (prev[:, 0:nw] - prw[:, 0:nw]) * par_ref[0:1, :]
    kr = prw[:, nw:2 * nw] + (prev[:, nw:2 * nw] - prw[:, nw:2 * nw]) * par_ref[1:2, :]
    vr = prw[:, 2 * nw:3 * nw] + (prev[:, 2 * nw:3 * nw] - prw[:, 2 * nw:3 * nw]) * par_ref[2:3, :]
    c0 = 3 * nw
    lo = prw[:, c0:c0 + 2 * LANES]
    lo = lo + (prev[:, c0:c0 + 2 * LANES] - lo) * par_ref[3:4, 0:2 * LANES]
    wa = lo[:, 0:LANES]
    lane = lax.broadcasted_iota(jnp.int32, wa.shape, 1)
    wa = jnp.where(lane < rank_w, jnp.tanh(wa), wa)
    proj = _mm1(wa, wwa_ref[...])
    w_log = -_softplus(-(par_ref[4:5, :] + proj[:, 0:nw])) - 0.5
    logdec = -jnp.exp(w_log)
    a = _sigmoid(par_ref[5:6, :] + proj[:, nw:2 * nw])
    gate = _mm1(_sigmoid(lo[:, LANES:2 * LANES]), g2_ref[...])
    kk = kr * par_ref[6:7, :]
    kk = kk * lax.rsqrt(_group_sum(kk * kk, bd) + L2_EPS)
    k2 = kr * (1.0 + (a - 1.0) * par_ref[7:8, :])
    return r, logdec, k2, vr, -kk, kk * a, gate


def _rw_epilogue(y, r, k2, v, gate, epi_ref, bd, n):
    mu = _group_sum(y, bd) * (1.0 / n)
    d = y - mu
    var = _group_sum(d * d, bd) * (1.0 / n)
    yn = d * lax.rsqrt(var + GN_EPS) * epi_ref[0:1, :] + epi_ref[1:2, :]
    bonus = _group_sum(r * k2 * epi_ref[2:3, :], bd) * v
    return (yn + bonus) * gate


def _rw_prompt_kernel(p1_ref, par_ref, wwa_ref, g2_ref, bd_ref, epi_ref,
                      o_ref, s_out_ref, s_ref, sbuf_ref, y_s, *, heads, n, rank_w):
    c = pl.program_id(1)
    nw = heads * n
    nrw = 3 * nw + 2 * LANES

    @pl.when(c == 0)
    def _():
        s_ref[...] = jnp.zeros_like(s_ref)
        sbuf_ref[0:8, :] = jnp.zeros((8, nrw), F32)

    prw = p1_ref[:, 0:nrw]
    sbuf_ref[8:8 + ROWS, :] = prw
    prev = sbuf_ref[7:7 + ROWS, :]
    sbuf_ref[0:8, :] = prw[ROWS - 8:ROWS, :]
    bd = bd_ref[...]
    r, logdec, k2, v, a_in, b_in, gate = _rw_prep(prw, prev, par_ref, wwa_ref, g2_ref, bd, nw, rank_w)
    low, _ = _chunk_masks()
    g_all = _mm_mask_lhs(low, logdec)
    ri, ci = _iota2(CHUNK, CHUNK)

    hcs = [(s, h) for s in range(ROWS // CHUNK) for h in range(heads)]
    at_l, rt_l, bt_l, kt_l, bd_l, kd_l, vv_l, egl_l = [], [], [], [], [], [], [], []
    for s, h in hcs:
        sl = (slice(s * CHUNK, (s + 1) * CHUNK), slice(h * n, (h + 1) * n))
        g = g_all[sl]
        eg = jnp.exp(g)
        ieg = jnp.exp(-g)
        gl = g[CHUNK - 1:CHUNK, :]
        tail = jnp.exp(gl - g)
        at_l.append(a_in[sl] * jnp.exp(g - logdec[sl]))
        rt_l.append(r[sl] * eg)
        bt_l.append(b_in[sl] * ieg)
        kt_l.append(k2[sl] * ieg)
        bd_l.append(b_in[sl] * tail)
        kd_l.append(k2[sl] * tail)
        vv_l.append(v[sl])
        egl_l.append(jnp.exp(gl))
    nab_l = [jnp.where(ci < ri, -x, 0.0) for x in _mm3_each(list(zip(at_l, bt_l)), NT)]
    aak_l = [jnp.where(ci < ri, x, 0.0) for x in _mm3_each(list(zip(at_l, kt_l)), NT)]
    rb_l = [jnp.where(ci <= ri, x, 0.0) for x in _mm3_each(list(zip(rt_l, bt_l)), NT)]
    rk_l = [jnp.where(ci <= ri, x, 0.0) for x in _mm3_each(list(zip(rt_l, kt_l)), NT)]
    t_l = _tri_inv_each(nab_l, ri, ci)
    akv_l = _mm3_each(list(zip(aak_l, vv_l)))
    rkv_l = _mm3_each(list(zip(rk_l, vv_l)))
    vkd_l = _mm3_each(list(zip(vv_l, kd_l)), TN)

    for s in range(ROWS // CHUNK):
        r0 = s * CHUNK
        ix = [s * heads + h for h in range(heads)]
        st_l = [s_ref[h] for h in range(heads)]
        as_l = _mm3_each([(at_l[i], st) for i, st in zip(ix, st_l)], NT)
        rs_l = _mm3_each([(rt_l[i], st) for i, st in zip(ix, st_l)], NT)
        uu_l = _mm3_each([(t_l[i], x + akv_l[i]) for i, x in zip(ix, as_l)])
        ru_l = _mm3_each([(rb_l[i], uu) for i, uu in zip(ix, uu_l)])
        ub_l = _mm3_each([(uu, bd_l[i]) for i, uu in zip(ix, uu_l)], TN)
        for h in range(heads):
            i = ix[h]
            y_s[r0:r0 + CHUNK, h * n:(h + 1) * n] = rs_l[h] + ru_l[h] + rkv_l[i]
            s_ref[h] = st_l[h] * egl_l[i] + ub_l[h] + vkd_l[i]

    o_ref[...] = _rw_epilogue(y_s[...], r, k2, v, gate, epi_ref, bd, n).astype(o_ref.dtype)

    @pl.when(c == pl.num_programs(1) - 1)
    def _():
        s_out_ref[0] = s_ref[...]


def _rw_prompt(p, par, wwa, g2, bd, epi, batch, nblk, heads, n, rank_w):
    m = p.shape[0]
    nw = heads * n
    nrw = 3 * nw + 2 * LANES
    kern = functools.partial(_rw_prompt_kernel, heads=heads, n=n, rank_w=rank_w)
    return pl.pallas_call(
        kern,
        grid=(batch, nblk),
        in_specs=[
            pl.BlockSpec((ROWS, 2048), lambda b, c: (b * nblk + c, 1)),
            pl.BlockSpec((8, nw), lambda b, c: (0, 0)),
            pl.BlockSpec((LANES, 2 * nw), lambda b, c: (0, 0)),
            pl.BlockSpec((LANES, nw), lambda b, c: (0, 0)),
            pl.BlockSpec((nw, nw), lambda b, c: (0, 0)),
            pl.BlockSpec((8, nw), lambda b, c: (0, 0)),
        ],
        out_specs=[
            pl.BlockSpec((ROWS, nw), lambda b, c: (b * nblk + c, 0)),
            pl.BlockSpec((1, heads, n, n), lambda b, c: (b, 0, 0, 0)),
        ],
        out_shape=[jax.ShapeDtypeStruct((m, nw), BF16),
                   jax.ShapeDtypeStruct((batch, heads, n, n), F32)],
        scratch_shapes=[pltpu.VMEM((heads, n, n), F32), pltpu.VMEM((8 + ROWS, nrw), F32),
                        pltpu.VMEM((ROWS, nw), F32)],
        compiler_params=pltpu.CompilerParams(
            dimension_semantics=("arbitrary", "arbitrary"), vmem_limit_bytes=VMEM_LIMIT),
        name="rw_prompt",
    )(p, par, wwa, g2, bd, epi)


def _rw_sample_kernel(p1_ref, prev_ref, par_ref, wwa_ref, g2_ref, bd_ref, epi_ref, s_in_ref,
                      o_ref, s_out_ref, r_s, w_s, k_s, v_s, a_s, b_s, y_s, *, heads, n, rank_w, tb):
    nw = heads * n
    nrw = 3 * nw + 2 * LANES
    bd = bd_ref[...]
    prw = p1_ref[:, 0:nrw]
    r, logdec, k2, v, a_in, b_in, gate = _rw_prep(prw, prev_ref[...], par_ref, wwa_ref, g2_ref, bd, nw, rank_w)
    r_s[...] = r
    w_s[...] = jnp.exp(logdec)
    k_s[...] = k2
    v_s[...] = v
    a_s[...] = a_in
    b_s[...] = b_in
    ri, ci = _iota2(n, n)
    eye = ri == ci

    for i in range(tb):
        for h in range(heads):
            sl = (slice(i, i + 1), slice(h * n, (h + 1) * n))
            st = s_in_ref[i, h]
            sa = jnp.sum(st * a_s[sl], axis=1, keepdims=True)
            v_col = _col_from_row(v_s[sl], eye)
            st = st * w_s[sl] + sa * b_s[sl] + v_col * k_s[sl]
            s_out_ref[i, h] = st
            y_col = jnp.sum(st * r_s[sl], axis=1, keepdims=True)
            y_s[sl] = _row_from_col(y_col, eye)
    o_ref[...] = _rw_epilogue(y_s[...], r, k2, v, gate, epi_ref, bd, n).astype(o_ref.dtype)


def _rw_sample(p, prev, par, wwa, g2, bd, epi, state, heads, n, rank_w, tb=8):
    m = p.shape[0]
    nw = heads * n
    nrw = 3 * nw + 2 * LANES
    kern = functools.partial(_rw_sample_kernel, heads=heads, n=n, rank_w=rank_w, tb=tb)
    return pl.pallas_call(
        kern,
        grid=(m // tb,),
        in_specs=[
            pl.BlockSpec((tb, 2048), lambda i: (i, 1)),
            pl.BlockSpec((tb, nrw), lambda i: (i, 0)),
            pl.BlockSpec((8, nw), lambda i: (0, 0)),
            pl.BlockSpec((LANES, 2 * nw), lambda i: (0, 0)),
            pl.BlockSpec((LANES, nw), lambda i: (0, 0)),
            pl.BlockSpec((nw, nw), lambda i: (0, 0)),
            pl.BlockSpec((8, nw), lambda i: (0, 0)),
            pl.BlockSpec((tb, heads, n, n), lambda i: (i, 0, 0, 0)),
        ],
        out_specs=[
            pl.BlockSpec((tb, nw), lambda i: (i, 0)),
            pl.BlockSpec((tb, heads, n, n), lambda i: (i, 0, 0, 0)),
        ],
        out_shape=[jax.ShapeDtypeStruct((m, nw), BF16), jax.ShapeDtypeStruct(state.shape, F32)],
        scratch_shapes=[pltpu.VMEM((tb, nw), F32)] * 7,
        compiler_params=pltpu.CompilerParams(
            dimension_semantics=("arbitrary",), vmem_limit_bytes=VMEM_LIMIT),
        name="rw_sample",
    )(p, prev, par, wwa, g2, bd, epi, state)


def _merge_kernel(x_ref, odn_ref, orw_ref, gt_ref, wd_ref, wr_ref, wo_ref, h_ref, *, d):
    ga = _sigmoid(gt_ref[:, 0:d])
    gb = _sigmoid(gt_ref[:, d:2 * d])
    m = ga * _dg(odn_ref[...], wd_ref[...]) + gb * _dg(orw_ref[...], wr_ref[...])
    h_ref[...] = x_ref[...] + _dg(m.astype(BF16), wo_ref[...])


def _merge(x, odn, orw, p, wd, wr, wo, batch, nblk_in, nblk_out, skip, tm):
    d = x.shape[1]
    nd = odn.shape[1]
    nr = orw.shape[1]
    kern = functools.partial(_merge_kernel, d=d)
    in_row = lambda b, i: (b * nblk_in + skip + i, 0)
    return pl.pallas_call(
        kern,
        grid=(batch, nblk_out),
        in_specs=[
            pl.BlockSpec((tm, d), in_row),
            pl.BlockSpec((tm, nd), in_row),
            pl.BlockSpec((tm, nr), in_row),
            pl.BlockSpec((tm, 2 * d), lambda b, i: (b * nblk_in + skip + i, 2)),
            pl.BlockSpec((nd, d), lambda b, i: (0, 0)),
            pl.BlockSpec((nr, d), lambda b, i: (0, 0)),
            pl.BlockSpec((d, d), lambda b, i: (0, 0)),
        ],
        out_specs=pl.BlockSpec((tm, d), lambda b, i: (b * nblk_out + i, 0)),
        out_shape=jax.ShapeDtypeStruct((batch * nblk_out * tm, d), F32),
        compiler_params=pltpu.CompilerParams(
            dimension_semantics=("arbitrary", "arbitrary"), vmem_limit_bytes=VMEM_LIMIT),
        name="merge",
    )(x, odn, orw, p, wd, wr, wo)


def _top_values(x, k):
    out = []
    cur = x
    for _ in range(k):
        m = jnp.max(cur, axis=0, keepdims=True)
        out.append(m)
        cur = jnp.where(cur == m, -jnp.inf, cur)
    return out


def _peer_kernel(h_ref, ln2_ref, wq_ref, keys_ref, u_ref, vt_ref, lnf_ref, y_ref,
                 xn_s, q_s, sc_s, top_s, e0_s, e1_s, tau_s, ht_s, g_s, acc_s, *, heads, nkeys, eb):
    e = pl.program_id(1)
    ne = pl.num_programs(1)
    k = PEER_TOPK

    @pl.when(e == 0)
    def _():
        x = h_ref[...]
        xn = x * lax.rsqrt(jnp.mean(x * x, -1, keepdims=True) + NORM_EPS) * ln2_ref[...]
        xn_s[...] = xn.astype(BF16)
        q = _dg(xn_s[...], wq_ref[...]).astype(BF16)
        for i in range(2 * heads):
            q_s[i] = q[:, i * nkeys:(i + 1) * nkeys]

        def score_body(i, carry):
            s = _dg(keys_ref[i], q_s[i], NT)
            sc_s[i] = s
            top_s[i] = jnp.concatenate(_top_values(s, k), axis=0)
            return carry

        lax.fori_loop(0, 2 * heads, score_body, 0)

        def head_body(hd, carry):
            a = top_s[2 * hd]
            b = top_s[2 * hd + 1]
            assert k == 16
            cand = jnp.concatenate(
                [a[0:1, :] + b, a[1:2, :] + b[0:8, :]] + [a[i:i + 1, :] + b[0:8, :] for i in range(2, 8)]
                + [a[8:16, :] + b[0:1, :]], axis=0)
            cv = _top_values(cand, k)
            z = jnp.exp(cv[0] - cv[0])
            for j in range(1, k):
                z = z + jnp.exp(cv[j] - cv[0])
            s0 = sc_s[2 * hd]
            e0_s[hd] = jnp.exp(s0 - a[0:1, :])
            e1_s[hd] = jnp.exp(sc_s[2 * hd + 1] - b[0:1, :]) / z
            tau = jnp.full(s0.shape, jnp.inf, F32)
            for j in range(k):
                tau = jnp.where(s0 + b[j:j + 1, :] >= cv[k - 1], b[j:j + 1, :], tau)
            tau_s[hd] = tau
            return carry

        lax.fori_loop(0, heads, head_body, 0)
        acc_s[...] = jnp.zeros_like(acc_s)

    ht_s[...] = _dg(u_ref[...], xn_s[...], NT)
    per = eb // nkeys
    assert per % 8 == 0
    i0 = pl.multiple_of(e * per, 8)
    for il in range(per):
        w = jnp.zeros((nkeys, ht_s.shape[1]), F32)
        for hd in range(heads):
            tau = tau_s[hd, pl.ds(i0, per), :]
            e0 = e0_s[hd, pl.ds(i0, per), :]
            w = w + jnp.where(sc_s[2 * hd + 1] >= tau[il:il + 1, :], e1_s[hd], 0.0) * e0[il:il + 1, :]
        g = jax.nn.gelu(ht_s[il * nkeys:(il + 1) * nkeys, :]) * w
        g_s[il * nkeys:(il + 1) * nkeys, :] = g.astype(BF16)
    acc_s[...] += _dg(vt_ref[...], g_s[...])

    @pl.when(e == ne - 1)
    def _():
        out = h_ref[...] + acc_s[...].T
        y_ref[...] = out * lax.rsqrt(jnp.mean(out * out, -1, keepdims=True) + NORM_EPS) * lnf_ref[...]


def _peer(h, ln2, wq, keys, u, vt, lnf, tb, eb=1024):
    m, d = h.shape
    nq = wq.shape[1]
    nk2, nkeys, half = keys.shape
    heads = nk2 // 2
    nexp = u.shape[0]
    kern = functools.partial(_peer_kernel, heads=heads, nkeys=nkeys, eb=eb)
    return pl.pallas_call(
        kern,
        grid=(m // tb, nexp // eb),
        in_specs=[
            pl.BlockSpec((tb, d), lambda t, e: (t, 0)),
            pl.BlockSpec((1, d), lambda t, e: (0, 0)),
            pl.BlockSpec((d, nq), lambda t, e: (0, 0)),
            pl.BlockSpec((nk2, nkeys, half), lambda t, e: (0, 0, 0)),
            pl.BlockSpec((eb, d), lambda t, e: (e, 0)),
            pl.BlockSpec((d, eb), lambda t, e: (0, e)),
            pl.BlockSpec((1, d), lambda t, e: (0, 0)),
        ],
        out_specs=pl.BlockSpec((tb, d), lambda t, e: (t, 0)),
        out_shape=jax.ShapeDtypeStruct((m, d), F32),
        scratch_shapes=[
            pltpu.VMEM((tb, d), BF16),
            pltpu.VMEM((nk2, tb, half), BF16),
            pltpu.VMEM((nk2, nkeys, tb), F32),
            pltpu.VMEM((nk2, PEER_TOPK, tb), F32),
            pltpu.VMEM((heads, nkeys, tb), F32),
            pltpu.VMEM((heads, nkeys, tb), F32),
            pltpu.VMEM((heads, nkeys, tb), F32),
            pltpu.VMEM((eb, tb), F32),
            pltpu.VMEM((eb, tb), BF16),
            pltpu.VMEM((d, tb), F32),
        ],
        compiler_params=pltpu.CompilerParams(
            dimension_semantics=("arbitrary", "arbitrary"), vmem_limit_bytes=VMEM_LIMIT),
        name="peer",
    )(h, ln2, wq, keys, u, vt, lnf)


def _pad_lanes(v, width, offset=0):
    out = jnp.zeros((width,), F32)
    return out.at[offset:offset + v.shape[0]].set(v.astype(F32))


def kernel(x_prompt, x_sample, cache_dn_conv, state_dn, cache_rw_shift, state_rw, meta_tokens, ln1, w_in, dn_conv_w, dn_a_log, dn_dt_bias, dn_norm_w, rw_mu, rw_w0, rw_w2, rw_a0, rw_a2, rw_g2, rw_k_k, rw_k_a, rw_r_k, rw_gn_w, rw_gn_b, w_up_dn, w_up_rw, w_out, ln2, peer_wq, peer_keys, peer_u, peer_v, ln_f):
    bp, seq, d = x_prompt.shape
    bs, seq_s, _ = x_sample.shape
    depth = w_in.shape[0]
    assert depth == 1 and seq_s == 1
    n_meta = meta_tokens.shape[0]
    _, _, heads_dn, dk, dv = state_dn.shape
    _, _, heads_rw, n_rw, _ = state_rw.shape
    n_conv = dn_conv_w.shape[2]
    nqkv = 2 * heads_dn * dk + heads_dn * dv
    assert n_conv == nqkv == 1536 and dn_conv_w.shape[1] == 4 and heads_dn * dv == 512
    nw = heads_rw * n_rw
    rank_w, rank_a, rank_g = rw_w2.shape[1], rw_a2.shape[1], rw_g2.shape[1]
    assert nw == 512 and rank_w + rank_a == LANES and rank_g == LANES
    n_rwp = 3 * nw + rank_w + rank_a + rank_g
    o_z = nqkv
    o_beta = o_z + heads_dn * dv
    o_alpha = o_beta + heads_dn
    o_rw = o_alpha + heads_dn
    o_gate = o_rw + n_rwp
    assert w_in.shape[2] == o_gate + 2 * d and 2 * heads_dn == 8

    wi = w_in[0]
    seg1 = jnp.concatenate([wi[:, o_rw:o_gate], wi[:, o_beta:o_rw],
                            jnp.zeros((d, 2048 - n_rwp - 2 * heads_dn), F32)], axis=1)
    w_packed = jnp.concatenate([wi[:, 0:o_beta], seg1, wi[:, o_gate:]], axis=1).astype(BF16)
    w_bt = wi[:, o_beta:o_rw].T.astype(BF16)
    g1 = ln1[0][None, :]
    cw = dn_conv_w[0]
    prow = jnp.stack([_pad_lanes(dn_a_log[0], LANES, heads_dn), _pad_lanes(dn_dt_bias[0], LANES, heads_dn)])
    pcol = jnp.stack([_pad_lanes(dn_a_log[0], 8, heads_dn), _pad_lanes(dn_dt_bias[0], 8, heads_dn)], axis=1)
    nwd = dn_norm_w[0][None, :]
    mu = rw_mu[0]
    par = jnp.stack([mu[0:nw], mu[nw:2 * nw], mu[2 * nw:3 * nw], _pad_lanes(mu[3 * nw:], nw),
                     rw_w0[0], rw_a0[0], rw_k_k[0], rw_k_a[0]])
    wwa = jnp.zeros((LANES, 2 * nw), F32)
    wwa = wwa.at[0:rank_w, 0:nw].set(rw_w2[0]).at[rank_w:, nw:].set(rw_a2[0]).astype(BF16)
    g2 = rw_g2[0].astype(BF16)
    grp = jnp.arange(nw) // n_rw
    bd = (grp[:, None] == grp[None, :]).astype(BF16)
    epi = jnp.zeros((8, nw), F32).at[0].set(rw_gn_w[0]).at[1].set(rw_gn_b[0]).at[2].set(rw_r_k[0].reshape(-1))
    wd = w_up_dn[0].astype(BF16)
    wr = w_up_rw[0].astype(BF16)
    wo = w_out[0].astype(BF16)
    l2 = ln2[0][None, :]
    wq = peer_wq[0].astype(BF16)
    pk = peer_keys[0]
    keys = pk.reshape(pk.shape[0] * pk.shape[1], pk.shape[2], pk.shape[3]).astype(BF16)
    pu = peer_u[0].astype(BF16)
    pvt = peer_v[0].T.astype(BF16)
    lf = ln_f[None, :]

    t_real = n_meta + seq
    nblk = -(-t_real // ROWS)
    t_pad = nblk * ROWS
    n_front = t_pad - t_real
    assert (n_front + n_meta) % ROWS == 0 and seq % ROWS == 0
    skip = (n_front + n_meta) // ROWS
    xp = jnp.concatenate([jnp.zeros((bp, n_front, d), F32),
                          jnp.broadcast_to(meta_tokens.astype(F32), (bp, n_meta, d)), x_prompt], axis=1)
    xp = xp.reshape(bp * t_pad, d)
    tm = max(t for t in (1024, 512, 256, ROWS) if (bp * t_pad) % t == 0)
    pp, batp = _norm_proj(xp, g1, w_packed, w_bt, tm)
    odn_p, p_dn = _dn_prompt(pp, batp, cw, prow, pcol, nwd, bp, nblk, heads_dn, dk, dv)
    orw_p, p_rw = _rw_prompt(pp, par, wwa, g2, bd, epi, bp, nblk, heads_rw, n_rw, rank_w)
    hp = _merge(xp, odn_p, orw_p, pp, wd, wr, wo, bp, nblk, seq // ROWS, skip, ROWS)
    y_prompt = _peer(hp, l2, wq, keys, pu, pvt, lf, 512).reshape(bp, seq, d)
    pp3 = pp.reshape(bp, t_pad, -1)
    p_conv = pp3[:, t_pad - 3:, 0:nqkv]
    p_shift = pp3[:, t_pad - 1:, 2048:2048 + n_rwp]

    xs = x_sample.reshape(bs, d)
    ps, _ = _norm_proj(xs, g1, w_packed, w_bt, bs)
    cache_t = jnp.moveaxis(cache_dn_conv[0], 1, 0)
    odn_s, s_dn = _dn_sample(ps, cache_t, cw, prow, nwd, state_dn[0], heads_dn, dk, dv)
    orw_s, s_rw = _rw_sample(ps, cache_rw_shift[0][:, 0, :], par, wwa, g2, bd, epi, state_rw[0],
                             heads_rw, n_rw, rank_w)
    hs = _merge(xs, odn_s, orw_s, ps, wd, wr, wo, 1, 1, 1, 0, bs)
    y_sample = _peer(hs, l2, wq, keys, pu, pvt, lf, bs).reshape(bs, 1, d)
    s_conv = jnp.concatenate([cache_dn_conv[0][:, 1:], ps[:, None, 0:nqkv]], axis=1)
    s_shift = ps[:, None, 2048:2048 + n_rwp]

    return (y_prompt, y_sample, p_conv[None], p_dn[None], p_shift[None], p_rw[None],
            s_conv[None], s_dn[None], s_shift[None], s_rw[None])
```

```python
import functools

import jax
import jax.numpy as jnp
from jax import lax
from jax.experimental import pallas as pl
from jax.experimental.pallas import tpu as pltpu

F32 = jnp.float32
BF16 = jnp.bfloat16

NORM_EPS = 1e-6
L2_EPS = 1e-6
GN_EPS = 64e-5
PEER_TOPK = 16

LANES = 128
CHUNK = 64
ROWS = 2 * CHUNK
INV_BLOCK = 16
VMEM_LIMIT = 52 * 1024 * 1024

NN = ((1,), (0,))
NT = ((1,), (1,))
TN = ((0,), (0,))


def _dg(a, b, dims=NN):
    return lax.dot_general(a, b, (dims, ((), ())), preferred_element_type=F32)


def _mm1(a, b, dims=NN):
    return _dg(a.astype(BF16), b.astype(BF16), dims)


def _hi_lo(x):
    hi = x.astype(BF16)
    lo = (x - hi.astype(F32)).astype(BF16)
    return hi, lo


def _mm3(a, b, dims=NN):
    a1, a2 = _hi_lo(a)
    b1, b2 = _hi_lo(b)
    return _dg(a1, b1, dims) + (_dg(a1, b2, dims) + _dg(a2, b1, dims))


def _split3(x):
    x1 = x.astype(BF16)
    r1 = x - x1.astype(F32)
    x2 = r1.astype(BF16)
    x3 = (r1 - x2.astype(F32)).astype(BF16)
    return x1, x2, x3


def _mm_mask_lhs(m, x, dims=NN):
    x1, x2, x3 = _split3(x)
    m = m.astype(BF16)
    return _dg(m, x1, dims) + (_dg(m, x2, dims) + _dg(m, x3, dims))


def _mm_mask_rhs(x, m, dims=NN):
    x1, x2, x3 = _split3(x)
    m = m.astype(BF16)
    return _dg(x1, m, dims) + (_dg(x2, m, dims) + _dg(x3, m, dims))


def _sigmoid(x):
    return jax.nn.sigmoid(x)


def _silu(x):
    return x * jax.nn.sigmoid(x)


def _softplus(x):
    return jnp.maximum(x, 0.0) + jnp.log(1.0 + jnp.exp(-jnp.abs(x)))


def _iota2(n, m):
    return (lax.broadcasted_iota(jnp.int32, (n, m), 0), lax.broadcasted_iota(jnp.int32, (n, m), 1))


def _mm3_each(pairs, dims=NN):
    split = [(_hi_lo(a), _hi_lo(b)) for a, b in pairs]
    return [_dg(a1, b1, dims) + (_dg(a1, b2, dims) + _dg(a2, b1, dims)) for (a1, a2), (b1, b2) in split]


def _mm1_each(pairs, dims=NN):
    cast = [(a.astype(BF16), b.astype(BF16)) for a, b in pairs]
    return [_dg(a, b, dims) for a, b in cast]


def _tri_inv_each(mats, ri, ci):
    assert mats[0].shape == (CHUNK, CHUNK) and CHUNK == 4 * INV_BLOCK and INV_BLOCK == 16
    eye = jnp.where(ri == ci, 1.0, 0.0).astype(F32)
    same = lax.shift_right_logical(ri, 4) == lax.shift_right_logical(ci, 4)
    n = [jnp.where(same, a, 0.0) for a in mats]
    e = [a - x for a, x in zip(mats, n)]
    n2 = _mm3_each(list(zip(n, n)))
    n4 = _mm3_each(list(zip(n2, n2)))
    n8 = _mm3_each(list(zip(n4, n4)))
    d = _mm3_each([(eye - x, eye + y) for x, y in zip(n, n2)])
    d = _mm3_each([(x, eye + y) for x, y in zip(d, n4)])
    d = _mm3_each([(x, eye + y) for x, y in zip(d, n8)])
    f = _mm3_each(list(zip(d, e)))
    f2 = _mm3_each(list(zip(f, f)))
    x = _mm3_each([(eye - p, eye + q) for p, q in zip(f, f2)])
    return _mm3_each(list(zip(x, d)))


def _chunk_masks():
    ri, ci = _iota2(ROWS, ROWS)
    same = lax.shift_right_logical(ri, 6) == lax.shift_right_logical(ci, 6)
    low = jnp.where(same, jnp.where(ci <= ri, 1.0, 0.0), 0.0).astype(BF16)
    up = jnp.where(same, jnp.where(ri <= ci, 1.0, 0.0), 0.0).astype(BF16)
    return low, up


def _norm_proj_kernel(x_ref, g_ref, w_ref, wbt_ref, o_ref, bt_ref, xn_ref):
    @pl.when(pl.program_id(1) == 0)
    def _():
        x = x_ref[...]
        xn = x * lax.rsqrt(jnp.mean(x * x, -1, keepdims=True) + NORM_EPS) * g_ref[...]
        xn_ref[...] = xn.astype(BF16)
        bt_ref[...] = _dg(wbt_ref[...], xn_ref[...], NT)
    o_ref[...] = _dg(xn_ref[...], w_ref[...])


def _norm_proj(x, g, w_packed, w_bt, tm):
    m, d = x.shape
    n = w_packed.shape[1]
    tn = 2048
    return pl.pallas_call(
        _norm_proj_kernel,
        grid=(m // tm, n // tn),
        in_specs=[
            pl.BlockSpec((tm, d), lambda i, j: (i, 0)),
            pl.BlockSpec((1, d), lambda i, j: (0, 0)),
            pl.BlockSpec((d, tn), lambda i, j: (0, j)),
            pl.BlockSpec((8, d), lambda i, j: (0, 0)),
        ],
        out_specs=[
            pl.BlockSpec((tm, tn), lambda i, j: (i, j)),
            pl.BlockSpec((8, tm), lambda i, j: (0, i)),
        ],
        out_shape=[jax.ShapeDtypeStruct((m, n), F32), jax.ShapeDtypeStruct((8, m), F32)],
        scratch_shapes=[pltpu.VMEM((tm, d), BF16)],
        compiler_params=pltpu.CompilerParams(
            dimension_semantics=("arbitrary", "arbitrary"), vmem_limit_bytes=VMEM_LIMIT),
        name="norm_proj",
    )(x, g, w_packed, w_bt)


def _l2norm_rows(x):
    return x * lax.rsqrt(jnp.sum(x * x, -1, keepdims=True) + L2_EPS)


def _dn_out_norm(o, z, nw):
    on = o * lax.rsqrt(jnp.mean(o * o, -1, keepdims=True) + NORM_EPS) * nw
    return on * _silu(z)


def _dn_prompt_kernel(p0_ref, ba_ref, bat_ref, cw_ref, prow_ref, pcol_ref, nw_ref,
                      o_ref, s_out_ref, s_ref, cbuf_ref, *, heads, dk, dv):
    c = pl.program_id(1)
    nqk = heads * dk
    nconv = 2 * nqk + heads * dv

    @pl.when(c == 0)
    def _():
        s_ref[...] = jnp.zeros_like(s_ref)
        cbuf_ref[0:8, :] = jnp.zeros((8, nconv), F32)

    u = p0_ref[:, 0:nconv]
    cbuf_ref[8:8 + ROWS, :] = u
    y = (cw_ref[3:4, :] * u + cw_ref[2:3, :] * cbuf_ref[7:7 + ROWS, :]
         + cw_ref[1:2, :] * cbuf_ref[6:6 + ROWS, :] + cw_ref[0:1, :] * cbuf_ref[5:5 + ROWS, :])
    cbuf_ref[0:8, :] = u[ROWS - 8:ROWS, :]
    qkv = _silu(y)

    ba = ba_ref[...]
    beta_all = _sigmoid(ba)
    g_col = -jnp.exp(prow_ref[0:1, :]) * _softplus(ba + prow_ref[1:2, :])
    bat = bat_ref[...]
    g_row = -jnp.exp(pcol_ref[:, 0:1]) * _softplus(bat + pcol_ref[:, 1:2])
    low, up = _chunk_masks()
    gc_all = _mm_mask_lhs(low, g_col)
    gr_all = _mm_mask_rhs(g_row, up)
    ri, ci = _iota2(CHUNK, CHUNK)
    scale = dk ** -0.5

    hcs = [(s, h) for s in range(ROWS // CHUNK) for h in range(heads)]
    q_l, k_l, kb_l, vb_l, gc_l, dec_l = [], [], [], [], [], []
    for s, h in hcs:
        r0 = s * CHUNK
        q_l.append(_l2norm_rows(qkv[r0:r0 + CHUNK, h * dk:(h + 1) * dk]) * scale)
        k = _l2norm_rows(qkv[r0:r0 + CHUNK, nqk + h * dk:nqk + (h + 1) * dk])
        beta = beta_all[r0:r0 + CHUNK, h:h + 1]
        gc = gc_all[r0:r0 + CHUNK, heads + h:heads + h + 1]
        gr = gr_all[heads + h:heads + h + 1, r0:r0 + CHUNK]
        k_l.append(k)
        kb_l.append(k * beta)
        vb_l.append(qkv[r0:r0 + CHUNK, 2 * nqk + h * dv:2 * nqk + (h + 1) * dv] * beta)
        gc_l.append(gc)
        dec_l.append(jnp.where(ci <= ri, jnp.exp(jnp.minimum(gc - gr, 0.0)), 0.0))
    a_l = [jnp.where(ci < ri, x * dec, 0.0)
           for x, dec in zip(_mm1_each(list(zip(kb_l, k_l)), NT), dec_l)]
    t_l = _tri_inv_each(a_l, ri, ci)
    eg_l = [jnp.exp(gc) for gc in gc_l]
    uu_l = _mm1_each(list(zip(t_l, vb_l)))
    ww_l = _mm1_each([(t, kb * eg) for t, kb, eg in zip(t_l, kb_l, eg_l)])
    qk_l = [jnp.where(ci <= ri, x * dec, 0.0)
            for x, dec in zip(_mm1_each(list(zip(q_l, k_l)), NT), dec_l)]

    for s in range(ROWS // CHUNK):
        r0 = s * CHUNK
        ix = [s * heads + h for h in range(heads)]
        st_l = [s_ref[h] for h in range(heads)]
        ws_l = _mm1_each([(ww_l[i], st) for i, st in zip(ix, st_l)])
        qs_l = _mm1_each([(q_l[i] * eg_l[i], st) for i, st in zip(ix, st_l)])
        vn_l = [uu_l[i] - ws for i, ws in zip(ix, ws_l)]
        qv_l = _mm1_each([(qk_l[i], vn) for i, vn in zip(ix, vn_l)])
        gl_l = [gc_l[i][CHUNK - 1:CHUNK, :] for i in ix]
        kv_l = _mm1_each([(k_l[i] * jnp.exp(gl - gc_l[i]), vn) for i, gl, vn in zip(ix, gl_l, vn_l)], TN)
        for h in range(heads):
            s_ref[h] = st_l[h] * jnp.exp(gl_l[h]) + kv_l[h]
            z = p0_ref[r0:r0 + CHUNK, nconv + h * dv:nconv + (h + 1) * dv]
            o_ref[r0:r0 + CHUNK, h * dv:(h + 1) * dv] = _dn_out_norm(
                qs_l[h] + qv_l[h], z, nw_ref[...]).astype(o_ref.dtype)

    @pl.when(c == pl.num_programs(1) - 1)
    def _():
        s_out_ref[0] = s_ref[...]


def _dn_prompt(p, bat, cw, prow, pcol, nw, batch, nblk, heads, dk, dv):
    m = p.shape[0]
    nconv = 2 * heads * dk + heads * dv
    ba_blk = (2048 + 1792) // LANES
    kern = functools.partial(_dn_prompt_kernel, heads=heads, dk=dk, dv=dv)
    return pl.pallas_call(
        kern,
        grid=(batch, nblk),
        in_specs=[
            pl.BlockSpec((ROWS, 2048), lambda b, c: (b * nblk + c, 0)),
            pl.BlockSpec((ROWS, LANES), lambda b, c: (b * nblk + c, ba_blk)),
            pl.BlockSpec((8, ROWS), lambda b, c: (0, b * nblk + c)),
            pl.BlockSpec((4, nconv), lambda b, c: (0, 0)),
            pl.BlockSpec((2, LANES), lambda b, c: (0, 0)),
            pl.BlockSpec((8, 2), lambda b, c: (0, 0)),
            pl.BlockSpec((1, dv), lambda b, c: (0, 0)),
        ],
        out_specs=[
            pl.BlockSpec((ROWS, heads * dv), lambda b, c: (b * nblk + c, 0)),
            pl.BlockSpec((1, heads, dk, dv), lambda b, c: (b, 0, 0, 0)),
        ],
        out_shape=[jax.ShapeDtypeStruct((m, heads * dv), BF16),
                   jax.ShapeDtypeStruct((batch, heads, dk, dv), F32)],
        scratch_shapes=[pltpu.VMEM((heads, dk, dv), F32), pltpu.VMEM((8 + ROWS, nconv), F32)],
        compiler_params=pltpu.CompilerParams(
            dimension_semantics=("arbitrary", "arbitrary"), vmem_limit_bytes=VMEM_LIMIT),
        name="dn_prompt",
    )(p, p, bat, cw, prow, pcol, nw)


def _col_from_row(row, eye):
    n = row.shape[1]
    return jnp.sum(jnp.where(eye, jnp.broadcast_to(row, (n, n)), 0.0), axis=1, keepdims=True)


def _row_from_col(col, eye):
    n = col.shape[0]
    return jnp.sum(jnp.where(eye, jnp.broadcast_to(col, (n, n)), 0.0), axis=0, keepdims=True)


def _dn_sample_kernel(p0_ref, ba_ref, cache_ref, cw_ref, prow_ref, nw_ref, s_in_ref,
                      o_ref, s_out_ref, q_s, k_s, v_s, g_s, b_s, o_s, *, heads, dk, dv, tb):
    nqk = heads * dk
    nconv = 2 * nqk + heads * dv
    u = p0_ref[:, 0:nconv]
    y = (cw_ref[3:4, :] * u + cw_ref[2:3, :] * cache_ref[2] + cw_ref[1:2, :] * cache_ref[1]
         + cw_ref[0:1, :] * cache_ref[0])
    qkv = _silu(y)
    ba = ba_ref[...]
    b_s[...] = _sigmoid(ba)
    g_s[...] = -jnp.exp(prow_ref[0:1, :]) * _softplus(ba + prow_ref[1:2, :])
    scale = dk ** -0.5
    for h in range(heads):
        q_s[:, h * dk:(h + 1) * dk] = _l2norm_rows(qkv[:, h * dk:(h + 1) * dk]) * scale
        k_s[:, h * dk:(h + 1) * dk] = _l2norm_rows(qkv[:, nqk + h * dk:nqk + (h + 1) * dk])
    v_s[...] = qkv[:, 2 * nqk:]
    ri, ci = _iota2(dk, dk)
    eye = ri == ci

    for i in range(tb):
        for h in range(heads):
            k_row = k_s[i:i + 1, h * dk:(h + 1) * dk]
            q_row = q_s[i:i + 1, h * dk:(h + 1) * dk]
            v_row = v_s[i:i + 1, h * dv:(h + 1) * dv]
            g = g_s[i:i + 1, heads + h:heads + h + 1]
            beta = b_s[i:i + 1, h:h + 1]
            k_col = _col_from_row(k_row, eye)
            q_col = _col_from_row(q_row, eye)
            st = s_in_ref[i, h] * jnp.exp(g)
            kv = jnp.sum(st * k_col, axis=0, keepdims=True)
            delta = (v_row - kv) * beta
            st = st + k_col * delta
            s_out_ref[i, h] = st
            o_s[i:i + 1, h * dv:(h + 1) * dv] = jnp.sum(st * q_col, axis=0, keepdims=True)
    for h in range(heads):
        z = p0_ref[:, nconv + h * dv:nconv + (h + 1) * dv]
        o_ref[:, h * dv:(h + 1) * dv] = _dn_out_norm(
            o_s[:, h * dv:(h + 1) * dv], z, nw_ref[...]).astype(o_ref.dtype)


def _dn_sample(p, cache_t, cw, prow, nw, state, heads, dk, dv, tb=8):
    m = p.shape[0]
    nconv = 2 * heads * dk + heads * dv
    ba_blk = (2048 + 1792) // LANES
    kern = functools.partial(_dn_sample_kernel, heads=heads, dk=dk, dv=dv, tb=tb)
    return pl.pallas_call(
        kern,
        grid=(m // tb,),
        in_specs=[
            pl.BlockSpec((tb, 2048), lambda i: (i, 0)),
            pl.BlockSpec((tb, LANES), lambda i: (i, ba_blk)),
            pl.BlockSpec((3, tb, nconv), lambda i: (0, i, 0)),
            pl.BlockSpec((4, nconv), lambda i: (0, 0)),
            pl.BlockSpec((2, LANES), lambda i: (0, 0)),
            pl.BlockSpec((1, dv), lambda i: (0, 0)),
            pl.BlockSpec((tb, heads, dk, dv), lambda i: (i, 0, 0, 0)),
        ],
        out_specs=[
            pl.BlockSpec((tb, heads * dv), lambda i: (i, 0)),
            pl.BlockSpec((tb, heads, dk, dv), lambda i: (i, 0, 0, 0)),
        ],
        out_shape=[jax.ShapeDtypeStruct((m, heads * dv), BF16),
                   jax.ShapeDtypeStruct(state.shape, F32)],
        scratch_shapes=[pltpu.VMEM((tb, heads * dk), F32), pltpu.VMEM((tb, heads * dk), F32),
                        pltpu.VMEM((tb, heads * dv), F32), pltpu.VMEM((tb, LANES), F32),
                        pltpu.VMEM((tb, LANES), F32), pltpu.VMEM((tb, heads * dv), F32)],
        compiler_params=pltpu.CompilerParams(
            dimension_semantics=("arbitrary",), vmem_limit_bytes=VMEM_LIMIT),
        name="dn_sample",
    )(p, p, cache_t, cw, prow, nw, state)


def _group_sum(x, bd):
    return _mm_mask_rhs(x, bd)


def _rw_prep(prw, prev, par_ref, wwa_ref, g2_ref, bd, nw, rank_w):
    r = prw[:, 0:nw] + (prev[:, 0:nw] - prw[:, 0:nw]) * par_ref[0:1, :]
    kr = prw[:, nw:2 * nw] + (prev[:, nw:2 * nw] - prw[:, nw:2 * nw]) * par_ref[1:2, :]
    vr = prw[:, 2 * nw:3 * nw] + (prev[:, 2 * nw:3 * nw] - prw[:, 2 * nw:3 * nw]) * par_ref[2:3, :]
    c0 = 3 * nw
    lo = prw[:, c0:c0 + 2 * LANES]
    lo = lo + (prev[:, c0:c0 + 2 * LANES] - lo) * par_ref[3:4, 0:2 * LANES]
    wa = lo[:, 0:LANES]
    lane = lax.broadcasted_iota(jnp.int32, wa.shape, 1)
    wa = jnp.where(lane < rank_w, jnp.tanh(wa), wa)
    proj = _mm1(wa, wwa_ref[...])
    w_log = -_softplus(-(par_ref[4:5, :] + proj[:, 0:nw])) - 0.5
    logdec = -jnp.exp(w_log)
    a = _sigmoid(par_ref[5:6, :] + proj[:, nw:2 * nw])
    gate = _mm1(_sigmoid(lo[:, LANES:2 * LANES]), g2_ref[...])
    kk = kr * par_ref[6:7, :]
    kk = kk * lax.rsqrt(_group_sum(kk * kk, bd) + L2_EPS)
    k2 = kr * (1.0 + (a - 1.0) * par_ref[7:8, :])
    return r, logdec, k2, vr, -kk, kk * a, gate


def _rw_epilogue(y, r, k2, v, gate, epi_ref, bd, n):
    mu = _group_sum(y, bd) * (1.0 / n)
    d = y - mu
    var = _group_sum(d * d, bd) * (1.0 / n)
    yn = d * lax.rsqrt(var + GN_EPS) * epi_ref[0:1, :] + epi_ref[1:2, :]
    bonus = _group_sum(r * k2 * epi_ref[2:3, :], bd) * v
    return (yn + bonus) * gate


def _rw_prompt_kernel(p1_ref, par_ref, wwa_ref, g2_ref, bd_ref, epi_ref,
                      o_ref, s_out_ref, s_ref, sbuf_ref, y_s, *, heads, n, rank_w):
    c = pl.program_id(1)
    nw = heads * n
    nrw = 3 * nw + 2 * LANES

    @pl.when(c == 0)
    def _():
        s_ref[...] = jnp.zeros_like(s_ref)
        sbuf_ref[0:8, :] = jnp.zeros((8, nrw), F32)

    prw = p1_ref[:, 0:nrw]
    sbuf_ref[8:8 + ROWS, :] = prw
    prev = sbuf_ref[7:7 + ROWS, :]
    sbuf_ref[0:8, :] = prw[ROWS - 8:ROWS, :]
    bd = bd_ref[...]
    r, logdec, k2, v, a_in, b_in, gate = _rw_prep(prw, prev, par_ref, wwa_ref, g2_ref, bd, nw, rank_w)
    low, _ = _chunk_masks()
    g_all = _mm_mask_lhs(low, logdec)
    ri, ci = _iota2(CHUNK, CHUNK)

    hcs = [(s, h) for s in range(ROWS // CHUNK) for h in range(heads)]
    at_l, rt_l, bt_l, kt_l, bd_l, kd_l, vv_l, egl_l = [], [], [], [], [], [], [], []
    for s, h in hcs:
        sl = (slice(s * CHUNK, (s + 1) * CHUNK), slice(h * n, (h + 1) * n))
        g = g_all[sl]
        eg = jnp.exp(g)
        ieg = jnp.exp(-g)
        gl = g[CHUNK - 1:CHUNK, :]
        tail = jnp.exp(gl - g)
        at_l.append(a_in[sl] * jnp.exp(g - logdec[sl]))
        rt_l.append(r[sl] * eg)
        bt_l.append(b_in[sl] * ieg)
        kt_l.append(k2[sl] * ieg)
        bd_l.append(b_in[sl] * tail)
        kd_l.append(k2[sl] * tail)
        vv_l.append(v[sl])
        egl_l.append(jnp.exp(gl))
    nab_l = [jnp.where(ci < ri, -x, 0.0) for x in _mm1_each(list(zip(at_l, bt_l)), NT)]
    aak_l = [jnp.where(ci < ri, x, 0.0) for x in _mm1_each(list(zip(at_l, kt_l)), NT)]
    rb_l = [jnp.where(ci <= ri, x, 0.0) for x in _mm1_each(list(zip(rt_l, bt_l)), NT)]
    rk_l = [jnp.where(ci <= ri, x, 0.0) for x in _mm1_each(list(zip(rt_l, kt_l)), NT)]
    t_l = _tri_inv_each(nab_l, ri, ci)
    akv_l = _mm1_each(list(zip(aak_l, vv_l)))
    rkv_l = _mm1_each(list(zip(rk_l, vv_l)))
    vkd_l = _mm1_each(list(zip(vv_l, kd_l)), TN)

    for s in range(ROWS // CHUNK):
        r0 = s * CHUNK
        ix = [s * heads + h for h in range(heads)]
        st_l = [s_ref[h] for h in range(heads)]
        as_l = _mm1_each([(at_l[i], st) for i, st in zip(ix, st_l)], NT)
        rs_l = _mm1_each([(rt_l[i], st) for i, st in zip(ix, st_l)], NT)
        uu_l = _mm1_each([(t_l[i], x + akv_l[i]) for i, x in zip(ix, as_l)])
        ru_l = _mm1_each([(rb_l[i], uu) for i, uu in zip(ix, uu_l)])
        ub_l = _mm1_each([(uu, bd_l[i]) for i, uu in zip(ix, uu_l)], TN)
        for h in range(heads):
            i = ix[h]
            y_s[r0:r0 + CHUNK, h * n:(h + 1) * n] = rs_l[h] + ru_l[h] + rkv_l[i]
            s_ref[h] = st_l[h] * egl_l[i] + ub_l[h] + vkd_l[i]

    o_ref[...] = _rw_epilogue(y_s[...], r, k2, v, gate, epi_ref, bd, n).astype(o_ref.dtype)

    @pl.when(c == pl.num_programs(1) - 1)
    def _():
        s_out_ref[0] = s_ref[...]


def _rw_prompt(p, par, wwa, g2, bd, epi, batch, nblk, heads, n, rank_w):
    m = p.shape[0]
    nw = heads * n
    nrw = 3 * nw + 2 * LANES
    kern = functools.partial(_rw_prompt_kernel, heads=heads, n=n, rank_w=rank_w)
    return pl.pallas_call(
        kern,
        grid=(batch, nblk),
        in_specs=[
            pl.BlockSpec((ROWS, 2048), lambda b, c: (b * nblk + c, 1)),
            pl.BlockSpec((8, nw), lambda b, c: (0, 0)),
            pl.BlockSpec((LANES, 2 * nw), lambda b, c: (0, 0)),
            pl.BlockSpec((LANES, nw), lambda b, c: (0, 0)),
            pl.BlockSpec((nw, nw), lambda b, c: (0, 0)),
            pl.BlockSpec((8, nw), lambda b, c: (0, 0)),
        ],
        out_specs=[
            pl.BlockSpec((ROWS, nw), lambda b, c: (b * nblk + c, 0)),
            pl.BlockSpec((1, heads, n, n), lambda b, c: (b, 0, 0, 0)),
        ],
        out_shape=[jax.ShapeDtypeStruct((m, nw), BF16),
                   jax.ShapeDtypeStruct((batch, heads, n, n), F32)],
        scratch_shapes=[pltpu.VMEM((heads, n, n), F32), pltpu.VMEM((8 + ROWS, nrw), F32),
                        pltpu.VMEM((ROWS, nw), F32)],
        compiler_params=pltpu.CompilerParams(
            dimension_semantics=("arbitrary", "arbitrary"), vmem_limit_bytes=VMEM_LIMIT),
        name="rw_prompt",
    )(p, par, wwa, g2, bd, epi)


def _rw_sample_kernel(p1_ref, prev_ref, par_ref, wwa_ref, g2_ref, bd_ref, epi_ref, s_in_ref,
                      o_ref, s_out_ref, r_s, w_s, k_s, v_s, a_s, b_s, y_s, *, heads, n, rank_w, tb):
    nw = heads * n
    nrw = 3 * nw + 2 * LANES
    bd = bd_ref[...]
    prw = p1_ref[:, 0:nrw]
    r, logdec, k2, v, a_in, b_in, gate = _rw_prep(prw, prev_ref[...], par_ref, wwa_ref, g2_ref, bd, nw, rank_w)
    r_s[...] = r
    w_s[...] = jnp.exp(logdec)
    k_s[...] = k2
    v_s[...] = v
    a_s[...] = a_in
    b_s[...] = b_in
    ri, ci = _iota2(n, n)
    eye = ri == ci

    group = 2
    for i0 in range(0, tb, group):
        items = [(i, h) for i in range(i0, i0 + group) for h in range(heads)]
        sls = [(slice(i, i + 1), slice(h * n, (h + 1) * n)) for i, h in items]
        st_l = [s_in_ref[i, h] for i, h in items]
        sa_l = [jnp.sum(st * a_s[sl], axis=1, keepdims=True) for st, sl in zip(st_l, sls)]
        vc_l = [_col_from_row(v_s[sl], eye) for sl in sls]
        st_l = [st * w_s[sl] + sa * b_s[sl] + vc * k_s[sl] for st, sa, vc, sl in zip(st_l, sa_l, vc_l, sls)]
        yc_l = [jnp.sum(st * r_s[sl], axis=1, keepdims=True) for st, sl in zip(st_l, sls)]
        for (i, h), sl, st, yc in zip(items, sls, st_l, yc_l):
            s_out_ref[i, h] = st
            y_s[sl] = _row_from_col(yc, eye)
    o_ref[...] = _rw_epilogue(y_s[...], r, k2, v, gate, epi_ref, bd, n).astype(o_ref.dtype)


def _rw_sample(p, prev, par, wwa, g2, bd, epi, state, heads, n, rank_w, tb=8):
    m = p.shape[0]
    nw = heads * n
    nrw = 3 * nw + 2 * LANES
    kern = functools.partial(_rw_sample_kernel, heads=heads, n=n, rank_w=rank_w, tb=tb)
    return pl.pallas_call(
        kern,
        grid=(m // tb,),
        in_specs=[
            pl.BlockSpec((tb, 2048), lambda i: (i, 1)),
            pl.BlockSpec((tb, nrw), lambda i: (i, 0)),
            pl.BlockSpec((8, nw), lambda i: (0, 0)),
            pl.BlockSpec((LANES, 2 * nw), lambda i: (0, 0)),
            pl.BlockSpec((LANES, nw), lambda i: (0, 0)),
            pl.BlockSpec((nw, nw), lambda i: (0, 0)),
            pl.BlockSpec((8, nw), lambda i: (0, 0)),
            pl.BlockSpec((tb, heads, n, n), lambda i: (i, 0, 0, 0)),
        ],
        out_specs=[
            pl.BlockSpec((tb, nw), lambda i: (i, 0)),
            pl.BlockSpec((tb, heads, n, n), lambda i: (i, 0, 0, 0)),
        ],
        out_shape=[jax.ShapeDtypeStruct((m, nw), BF16), jax.ShapeDtypeStruct(state.shape, F32)],
        scratch_shapes=[pltpu.VMEM((tb, nw), F32)] * 7,
        compiler_params=pltpu.CompilerParams(
            dimension_semantics=("arbitrary",), vmem_limit_bytes=VMEM_LIMIT),
        name="rw_sample",
    )(p, prev, par, wwa, g2, bd, epi, state)


def _merge_kernel(x_ref, odn_ref, orw_ref, gt_ref, wd_ref, wr_ref, wo_ref, h_ref, *, d):
    ga = _sigmoid(gt_ref[:, 0:d])
    gb = _sigmoid(gt_ref[:, d:2 * d])
    m = ga * _dg(odn_ref[...], wd_ref[...]) + gb * _dg(orw_ref[...], wr_ref[...])
    h_ref[...] = x_ref[...] + _dg(m.astype(BF16), wo_ref[...])


def _merge(x, odn, orw, p, wd, wr, wo, batch, nblk_in, nblk_out, skip, tm):
    d = x.shape[1]
    nd = odn.shape[1]
    nr = orw.shape[1]
    kern = functools.partial(_merge_kernel, d=d)
    in_row = lambda b, i: (b * nblk_in + skip + i, 0)
    return pl.pallas_call(
        kern,
        grid=(batch, nblk_out),
        in_specs=[
            pl.BlockSpec((tm, d), in_row),
            pl.BlockSpec((tm, nd), in_row),
            pl.BlockSpec((tm, nr), in_row),
            pl.BlockSpec((tm, 2 * d), lambda b, i: (b * nblk_in + skip + i, 2)),
            pl.BlockSpec((nd, d), lambda b, i: (0, 0)),
            pl.BlockSpec((nr, d), lambda b, i: (0, 0)),
            pl.BlockSpec((d, d), lambda b, i: (0, 0)),
        ],
        out_specs=pl.BlockSpec((tm, d), lambda b, i: (b * nblk_out + i, 0)),
        out_shape=jax.ShapeDtypeStruct((batch * nblk_out * tm, d), F32),
        compiler_params=pltpu.CompilerParams(
            dimension_semantics=("arbitrary", "arbitrary"), vmem_limit_bytes=VMEM_LIMIT),
        name="merge",
    )(x, odn, orw, p, wd, wr, wo)


def _top_values(x, k):
    out = []
    cur = x
    for _ in range(k):
        m = jnp.max(cur, axis=0, keepdims=True)
        out.append(m)
        cur = jnp.where(cur == m, -jnp.inf, cur)
    return out


def _peer_kernel(h_ref, ln2_ref, wq_ref, keys_ref, u_ref, vt_ref, lnf_ref, y_ref,
                 xn_s, q_s, sc_s, top_s, e0_s, e1_s, tau_s, ht0_s, ht1_s, g0_s, g1_s, acc_s,
                 *, heads, nkeys, eb, ne):
    e = pl.program_id(1)
    k = PEER_TOPK

    @pl.when(e == 0)
    def _():
        x = h_ref[...]
        xn = x * lax.rsqrt(jnp.mean(x * x, -1, keepdims=True) + NORM_EPS) * ln2_ref[...]
        xn_s[...] = xn.astype(BF16)
        q = _dg(xn_s[...], wq_ref[...]).astype(BF16)
        for i in range(2 * heads):
            q_s[i] = q[:, i * nkeys:(i + 1) * nkeys]

        lane_tiles = [slice(lt * LANES, (lt + 1) * LANES) for lt in range(h_ref.shape[0] // LANES)]

        def score_body(i, carry):
            s = _dg(keys_ref[i], q_s[i], NT)
            sc_s[i] = s
            for ls in lane_tiles:
                top_s[i, :, ls] = jnp.concatenate(_top_values(s[:, ls], k), axis=0)
            return carry

        lax.fori_loop(0, 2 * heads, score_body, 0)

        def head_body(hd, carry):
            for lt, ls in enumerate(lane_tiles):
                a = top_s[2 * hd, :, ls]
                b = top_s[2 * hd + 1, :, ls]
                assert k == 16
                cand = jnp.concatenate(
                    [a[0:1, :] + b, a[1:2, :] + b[0:8, :]] + [a[i:i + 1, :] + b[0:8, :] for i in range(2, 8)]
                    + [a[8:16, :] + b[0:1, :]], axis=0)
                cv = _top_values(cand, k)
                z = jnp.exp(cv[0] - cv[0])
                for j in range(1, k):
                    z = z + jnp.exp(cv[j] - cv[0])
                zi = 1.0 / z
                s0 = sc_s[2 * hd, :, ls]
                e0_s[hd, lt] = jnp.exp(s0 - a[0:1, :])
                e1_s[hd, lt] = jnp.exp(sc_s[2 * hd + 1, :, ls] - b[0:1, :]) * zi
                tau = jnp.full(s0.shape, jnp.inf, F32)
                for j in range(k):
                    bj = b[j:j + 1, :]
                    tau = jnp.where(s0 + bj >= cv[k - 1], jnp.exp(bj - b[0:1, :]) * zi, tau)
                tau_s[hd, lt] = tau
            return carry

        lax.fori_loop(0, heads, head_body, 0)
        acc_s[...] = jnp.zeros_like(acc_s)
        ht1_s[...] = jnp.zeros_like(ht1_s)
        g0_s[...] = jnp.zeros_like(g0_s)
        g1_s[...] = jnp.zeros_like(g1_s)

    per = eb // nkeys
    assert per % 8 == 0
    i0 = pl.multiple_of(jnp.clip(e - 1, 0, ne - 1) * per, 8)
    nlt = ht0_s.shape[0]
    mxw = min(2, nlt)

    def step(ht_w, ht_r, g_w, g_r):
        def pre_act(p):
            rows = slice(p * mxw * LANES, (p + 1) * mxw * LANES)
            res = _dg(u_ref[...], xn_s[rows, :], NT)
            for j in range(mxw):
                ht_w[p * mxw + j] = res[:, j * LANES:(j + 1) * LANES]

        def accumulate(p):
            cols = slice(p * mxw * LANES, (p + 1) * mxw * LANES)
            gp = jnp.concatenate([g_r[p * mxw + j] for j in range(mxw)], axis=1)
            acc_s[:, cols] += _dg(vt_ref[...], gp)

        def gate(lt):
            tau_l = [tau_s[hd, lt, pl.ds(i0, per), :] for hd in range(heads)]
            e0_l = [e0_s[hd, lt, pl.ds(i0, per), :] for hd in range(heads)]
            for il in range(per):
                w = jnp.zeros((nkeys, LANES), F32)
                for hd in range(heads):
                    e1 = e1_s[hd, lt]
                    w = w + jnp.where(e1 >= tau_l[hd][il:il + 1, :], e1, 0.0) * e0_l[hd][il:il + 1, :]
                g = jax.nn.gelu(ht_r[lt, il * nkeys:(il + 1) * nkeys, :]) * w
                g_w[lt, il * nkeys:(il + 1) * nkeys, :] = g.astype(BF16)

        for p in range(nlt // mxw):
            pre_act(p)
            gate(p * mxw)
            accumulate(p)
            for j in range(1, mxw):
                gate(p * mxw + j)

    parity = lax.rem(e, 2)

    @pl.when(parity == 0)
    def _():
        step(ht0_s, ht1_s, g1_s, g0_s)

    @pl.when(parity == 1)
    def _():
        step(ht1_s, ht0_s, g0_s, g1_s)

    @pl.when(e == ne + 1)
    def _():
        out = h_ref[...] + acc_s[...].T
        y_ref[...] = out * lax.rsqrt(jnp.mean(out * out, -1, keepdims=True) + NORM_EPS) * lnf_ref[...]


def _peer(h, ln2, wq, keys, u, vt, lnf, tb, eb=1024):
    m, d = h.shape
    nq = wq.shape[1]
    nk2, nkeys, half = keys.shape
    heads = nk2 // 2
    nexp = u.shape[0]
    ne = nexp // eb
    assert tb % LANES == 0
    nlt = tb // LANES
    kern = functools.partial(_peer_kernel, heads=heads, nkeys=nkeys, eb=eb, ne=ne)
    return pl.pallas_call(
        kern,
        grid=(m // tb, ne + 2),
        in_specs=[
            pl.BlockSpec((tb, d), lambda t, e: (t, 0)),
            pl.BlockSpec((1, d), lambda t, e: (0, 0)),
            pl.BlockSpec((d, nq), lambda t, e: (0, 0)),
            pl.BlockSpec((nk2, nkeys, half), lambda t, e: (0, 0, 0)),
            pl.BlockSpec((eb, d), lambda t, e: (jnp.minimum(e, ne - 1), 0)),
            pl.BlockSpec((d, eb), lambda t, e: (0, jnp.clip(e - 2, 0, ne - 1))),
            pl.BlockSpec((1, d), lambda t, e: (0, 0)),
        ],
        out_specs=pl.BlockSpec((tb, d), lambda t, e: (t, 0)),
        out_shape=jax.ShapeDtypeStruct((m, d), F32),
        scratch_shapes=[
            pltpu.VMEM((tb, d), BF16),
            pltpu.VMEM((nk2, tb, half), BF16),
            pltpu.VMEM((nk2, nkeys, tb), F32),
            pltpu.VMEM((nk2, PEER_TOPK, tb), F32),
            pltpu.VMEM((heads, nlt, nkeys, LANES), F32),
            pltpu.VMEM((heads, nlt, nkeys, LANES), F32),
            pltpu.VMEM((heads, nlt, nkeys, LANES), F32),
            pltpu.VMEM((nlt, eb, LANES), F32),
            pltpu.VMEM((nlt, eb, LANES), F32),
            pltpu.VMEM((nlt, eb, LANES), BF16),
            pltpu.VMEM((nlt, eb, LANES), BF16),
            pltpu.VMEM((d, tb), F32),
        ],
        compiler_params=pltpu.CompilerParams(
            dimension_semantics=("arbitrary", "arbitrary"), vmem_limit_bytes=VMEM_LIMIT),
        name="peer",
    )(h, ln2, wq, keys, u, vt, lnf)


def _pad_lanes(v, width, offset=0):
    out = jnp.zeros((width,), F32)
    return out.at[offset:offset + v.shape[0]].set(v.astype(F32))


def kernel(x_prompt, x_sample, cache_dn_conv, state_dn, cache_rw_shift, state_rw, meta_tokens, ln1, w_in, dn_conv_w, dn_a_log, dn_dt_bias, dn_norm_w, rw_mu, rw_w0, rw_w2, rw_a0, rw_a2, rw_g2, rw_k_k, rw_k_a, rw_r_k, rw_gn_w, rw_gn_b, w_up_dn, w_up_rw, w_out, ln2, peer_wq, peer_keys, peer_u, peer_v, ln_f):
    bp, seq, d = x_prompt.shape
    bs, seq_s, _ = x_sample.shape
    depth = w_in.shape[0]
    assert depth == 1 and seq_s == 1
    n_meta = meta_tokens.shape[0]
    _, _, heads_dn, dk, dv = state_dn.shape
    _, _, heads_rw, n_rw, _ = state_rw.shape
    n_conv = dn_conv_w.shape[2]
    nqkv = 2 * heads_dn * dk + heads_dn * dv
    assert n_conv == nqkv == 1536 and dn_conv_w.shape[1] == 4 and heads_dn * dv == 512
    nw = heads_rw * n_rw
    rank_w, rank_a, rank_g = rw_w2.shape[1], rw_a2.shape[1], rw_g2.shape[1]
    assert nw == 512 and rank_w + rank_a == LANES and rank_g == LANES
    n_rwp = 3 * nw + rank_w + rank_a + rank_g
    o_z = nqkv
    o_beta = o_z + heads_dn * dv
    o_alpha = o_beta + heads_dn
    o_rw = o_alpha + heads_dn
    o_gate = o_rw + n_rwp
    assert w_in.shape[2] == o_gate + 2 * d and 2 * heads_dn == 8

    wi = w_in[0]
    seg1 = jnp.concatenate([wi[:, o_rw:o_gate], wi[:, o_beta:o_rw],
                            jnp.zeros((d, 2048 - n_rwp - 2 * heads_dn), F32)], axis=1)
    w_packed = jnp.concatenate([wi[:, 0:o_beta], seg1, wi[:, o_gate:]], axis=1).astype(BF16)
    w_bt = wi[:, o_beta:o_rw].T.astype(BF16)
    g1 = ln1[0][None, :]
    cw = dn_conv_w[0]
    prow = jnp.stack([_pad_lanes(dn_a_log[0], LANES, heads_dn), _pad_lanes(dn_dt_bias[0], LANES, heads_dn)])
    pcol = jnp.stack([_pad_lanes(dn_a_log[0], 8, heads_dn), _pad_lanes(dn_dt_bias[0], 8, heads_dn)], axis=1)
    nwd = dn_norm_w[0][None, :]
    mu = rw_mu[0]
    par = jnp.stack([mu[0:nw], mu[nw:2 * nw], mu[2 * nw:3 * nw], _pad_lanes(mu[3 * nw:], nw),
                     rw_w0[0], rw_a0[0], rw_k_k[0], rw_k_a[0]])
    wwa = jnp.zeros((LANES, 2 * nw), F32)
    wwa = wwa.at[0:rank_w, 0:nw].set(rw_w2[0]).at[rank_w:, nw:].set(rw_a2[0]).astype(BF16)
    g2 = rw_g2[0].astype(BF16)
    grp = jnp.arange(nw) // n_rw
    bd = (grp[:, None] == grp[None, :]).astype(BF16)
    epi = jnp.zeros((8, nw), F32).at[0].set(rw_gn_w[0]).at[1].set(rw_gn_b[0]).at[2].set(rw_r_k[0].reshape(-1))
    wd = w_up_dn[0].astype(BF16)
    wr = w_up_rw[0].astype(BF16)
    wo = w_out[0].astype(BF16)
    l2 = ln2[0][None, :]
    wq = peer_wq[0].astype(BF16)
    pk = peer_keys[0]
    keys = pk.reshape(pk.shape[0] * pk.shape[1], pk.shape[2], pk.shape[3]).astype(BF16)
    pu = peer_u[0].astype(BF16)
    pvt = peer_v[0].T.astype(BF16)
    lf = ln_f[None, :]

    t_real = n_meta + seq
    nblk = -(-t_real // ROWS)
    t_pad = nblk * ROWS
    n_front = t_pad - t_real
    assert (n_front + n_meta) % ROWS == 0 and seq % ROWS == 0
    skip = (n_front + n_meta) // ROWS
    xp = jnp.concatenate([jnp.zeros((bp, n_front, d), F32),
                          jnp.broadcast_to(meta_tokens.astype(F32), (bp, n_meta, d)), x_prompt], axis=1)
    xp = xp.reshape(bp * t_pad, d)
    tm = max(t for t in (1024, 512, 256, ROWS) if (bp * t_pad) % t == 0)
    pp, batp = _norm_proj(xp, g1, w_packed, w_bt, tm)
    odn_p, p_dn = _dn_prompt(pp, batp, cw, prow, pcol, nwd, bp, nblk, heads_dn, dk, dv)
    orw_p, p_rw = _rw_prompt(pp, par, wwa, g2, bd, epi, bp, nblk, heads_rw, n_rw, rank_w)
    hp = _merge(xp, odn_p, orw_p, pp, wd, wr, wo, bp, nblk, seq // ROWS, skip, ROWS)
    y_prompt = _peer(hp, l2, wq, keys, pu, pvt, lf, 512).reshape(bp, seq, d)
    pp3 = pp.reshape(bp, t_pad, -1)
    p_conv = pp3[:, t_pad - 3:, 0:nqkv]
    p_shift = pp3[:, t_pad - 1:, 2048:2048 + n_rwp]

    xs = x_sample.reshape(bs, d)
    ps, _ = _norm_proj(xs, g1, w_packed, w_bt, bs)
    cache_t = jnp.moveaxis(cache_dn_conv[0], 1, 0)
    odn_s, s_dn = _dn_sample(ps, cache_t, cw, prow, nwd, state_dn[0], heads_dn, dk, dv)
    orw_s, s_rw = _rw_sample(ps, cache_rw_shift[0][:, 0, :], par, wwa, g2, bd, epi, state_rw[0],
                             heads_rw, n_rw, rank_w)
    hs = _merge(xs, odn_s, orw_s, ps, wd, wr, wo, 1, 1, 1, 0, bs)
    y_sample = _peer(hs, l2, wq, keys, pu, pvt, lf, bs).reshape(bs, 1, d)
    s_conv = jnp.concatenate([cache_dn_conv[0][:, 1:], ps[:, None, 0:nqkv]], axis=1)
    s_shift = ps[:, None, 2048:2048 + n_rwp]

    return (y_prompt, y_sample, p_conv[None], p_dn[None], p_shift[None], p_rw[None],
            s_conv[None], s_dn[None], s_shift[None], s_rw[None])
```

```python
import functools

import jax
import jax.numpy as jnp
from jax import lax
from jax.experimental import pallas as pl
from jax.experimental.pallas import tpu as pltpu

F32 = jnp.float32
BF16 = jnp.bfloat16

NORM_EPS = 1e-6
L2_EPS = 1e-6
GN_EPS = 64e-5
PEER_TOPK = 16

LANES = 128
CHUNK = 64
ROWS = 2 * CHUNK
INV_BLOCK = 16
VMEM_LIMIT = 52 * 1024 * 1024

NN = ((1,), (0,))
NT = ((1,), (1,))
TN = ((0,), (0,))


def _dg(a, b, dims=NN):
    return lax.dot_general(a, b, (dims, ((), ())), preferred_element_type=F32)


def _mm1(a, b, dims=NN):
    return _dg(a.astype(BF16), b.astype(BF16), dims)


def _hi_lo(x):
    hi = x.astype(BF16)
    lo = (x - hi.astype(F32)).astype(BF16)
    return hi, lo


def _mm3(a, b, dims=NN):
    a1, a2 = _hi_lo(a)
    b1, b2 = _hi_lo(b)
    return _dg(a1, b1, dims) + (_dg(a1, b2, dims) + _dg(a2, b1, dims))


def _split3(x):
    x1 = x.astype(BF16)
    r1 = x - x1.astype(F32)
    x2 = r1.astype(BF16)
    x3 = (r1 - x2.astype(F32)).astype(BF16)
    return x1, x2, x3


def _mm_mask_lhs(m, x, dims=NN):
    x1, x2, x3 = _split3(x)
    m = m.astype(BF16)
    return _dg(m, x1, dims) + (_dg(m, x2, dims) + _dg(m, x3, dims))


def _mm_mask_rhs(x, m, dims=NN):
    x1, x2, x3 = _split3(x)
    m = m.astype(BF16)
    return _dg(x1, m, dims) + (_dg(x2, m, dims) + _dg(x3, m, dims))


def _sigmoid(x):
    return jax.nn.sigmoid(x)


def _silu(x):
    return x * jax.nn.sigmoid(x)


def _softplus(x):
    return jnp.maximum(x, 0.0) + jnp.log(1.0 + jnp.exp(-jnp.abs(x)))


def _iota2(n, m):
    return (lax.broadcasted_iota(jnp.int32, (n, m), 0), lax.broadcasted_iota(jnp.int32, (n, m), 1))


def _mm3_each(pairs, dims=NN):
    split = [(_hi_lo(a), _hi_lo(b)) for a, b in pairs]
    return [_dg(a1, b1, dims) + (_dg(a1, b2, dims) + _dg(a2, b1, dims)) for (a1, a2), (b1, b2) in split]


def _mm1_each(pairs, dims=NN):
    cast = [(a.astype(BF16), b.astype(BF16)) for a, b in pairs]
    return [_dg(a, b, dims) for a, b in cast]


def _tri_inv_each(mats, ri, ci):
    assert mats[0].shape == (CHUNK, CHUNK) and CHUNK == 4 * INV_BLOCK and INV_BLOCK == 16
    eye = jnp.where(ri == ci, 1.0, 0.0).astype(F32)
    same = lax.shift_right_logical(ri, 4) == lax.shift_right_logical(ci, 4)
    n = [jnp.where(same, a, 0.0) for a in mats]
    e = [a - x for a, x in zip(mats, n)]
    n2 = _mm3_each(list(zip(n, n)))
    n4 = _mm3_each(list(zip(n2, n2)))
    n8 = _mm3_each(list(zip(n4, n4)))
    d = _mm3_each([(eye - x, eye + y) for x, y in zip(n, n2)])
    d = _mm3_each([(x, eye + y) for x, y in zip(d, n4)])
    d = _mm3_each([(x, eye + y) for x, y in zip(d, n8)])
    f = _mm3_each(list(zip(d, e)))
    f2 = _mm3_each(list(zip(f, f)))
    x = _mm3_each([(eye - p, eye + q) for p, q in zip(f, f2)])
    return _mm3_each(list(zip(x, d)))


def _chunk_masks():
    ri, ci = _iota2(ROWS, ROWS)
    same = lax.shift_right_logical(ri, 6) == lax.shift_right_logical(ci, 6)
    low = jnp.where(same, jnp.where(ci <= ri, 1.0, 0.0), 0.0).astype(BF16)
    up = jnp.where(same, jnp.where(ri <= ci, 1.0, 0.0), 0.0).astype(BF16)
    return low, up


def _norm_proj_kernel(x_ref, g_ref, w_ref, wbt_ref, o_ref, bt_ref, xn_ref):
    @pl.when(pl.program_id(1) == 0)
    def _():
        x = x_ref[...]
        xn = x * lax.rsqrt(jnp.mean(x * x, -1, keepdims=True) + NORM_EPS) * g_ref[...]
        xn_ref[...] = xn.astype(BF16)
        bt_ref[...] = _dg(wbt_ref[...], xn_ref[...], NT)
    o_ref[...] = _dg(xn_ref[...], w_ref[...])


def _norm_proj(x, g, w_packed, w_bt, tm):
    m, d = x.shape
    n = w_packed.shape[1]
    tn = 2048
    return pl.pallas_call(
        _norm_proj_kernel,
        grid=(m // tm, n // tn),
        in_specs=[
            pl.BlockSpec((tm, d), lambda i, j: (i, 0)),
            pl.BlockSpec((1, d), lambda i, j: (0, 0)),
            pl.BlockSpec((d, tn), lambda i, j: (0, j)),
            pl.BlockSpec((8, d), lambda i, j: (0, 0)),
        ],
        out_specs=[
            pl.BlockSpec((tm, tn), lambda i, j: (i, j)),
            pl.BlockSpec((8, tm), lambda i, j: (0, i)),
        ],
        out_shape=[jax.ShapeDtypeStruct((m, n), F32), jax.ShapeDtypeStruct((8, m), F32)],
        scratch_shapes=[pltpu.VMEM((tm, d), BF16)],
        compiler_params=pltpu.CompilerParams(
            dimension_semantics=("arbitrary", "arbitrary"), vmem_limit_bytes=VMEM_LIMIT),
        name="norm_proj",
    )(x, g, w_packed, w_bt)


def _l2norm_rows(x):
    return x * lax.rsqrt(jnp.sum(x * x, -1, keepdims=True) + L2_EPS)


def _dn_out_norm(o, z, nw):
    on = o * lax.rsqrt(jnp.mean(o * o, -1, keepdims=True) + NORM_EPS) * nw
    return on * _silu(z)


def _dn_prompt_kernel(p0_ref, ba_ref, bat_ref, cw_ref, prow_ref, pcol_ref, nw_ref,
                      o_ref, s_out_ref, s_ref, cbuf_ref, *, heads, dk, dv):
    c = pl.program_id(1)
    nqk = heads * dk
    nconv = 2 * nqk + heads * dv

    @pl.when(c == 0)
    def _():
        s_ref[...] = jnp.zeros_like(s_ref)
        cbuf_ref[0:8, :] = jnp.zeros((8, nconv), F32)

    u = p0_ref[:, 0:nconv]
    cbuf_ref[8:8 + ROWS, :] = u
    y = (cw_ref[3:4, :] * u + cw_ref[2:3, :] * cbuf_ref[7:7 + ROWS, :]
         + cw_ref[1:2, :] * cbuf_ref[6:6 + ROWS, :] + cw_ref[0:1, :] * cbuf_ref[5:5 + ROWS, :])
    cbuf_ref[0:8, :] = u[ROWS - 8:ROWS, :]
    qkv = _silu(y)

    ba = ba_ref[...]
    beta_all = _sigmoid(ba)
    g_col = -jnp.exp(prow_ref[0:1, :]) * _softplus(ba + prow_ref[1:2, :])
    bat = bat_ref[...]
    g_row = -jnp.exp(pcol_ref[:, 0:1]) * _softplus(bat + pcol_ref[:, 1:2])
    low, up = _chunk_masks()
    gc_all = _mm_mask_lhs(low, g_col)
    gr_all = _mm_mask_rhs(g_row, up)
    ri, ci = _iota2(CHUNK, CHUNK)
    scale = dk ** -0.5

    hcs = [(s, h) for s in range(ROWS // CHUNK) for h in range(heads)]
    q_l, k_l, kb_l, vb_l, gc_l, dec_l = [], [], [], [], [], []
    for s, h in hcs:
        r0 = s * CHUNK
        q_l.append(_l2norm_rows(qkv[r0:r0 + CHUNK, h * dk:(h + 1) * dk]) * scale)
        k = _l2norm_rows(qkv[r0:r0 + CHUNK, nqk + h * dk:nqk + (h + 1) * dk])
        beta = beta_all[r0:r0 + CHUNK, h:h + 1]
        gc = gc_all[r0:r0 + CHUNK, heads + h:heads + h + 1]
        gr = gr_all[heads + h:heads + h + 1, r0:r0 + CHUNK]
        k_l.append(k)
        kb_l.append(k * beta)
        vb_l.append(qkv[r0:r0 + CHUNK, 2 * nqk + h * dv:2 * nqk + (h + 1) * dv] * beta)
        gc_l.append(gc)
        dec_l.append(jnp.where(ci <= ri, jnp.exp(jnp.minimum(gc - gr, 0.0)), 0.0))
    a_l = [jnp.where(ci < ri, x * dec, 0.0)
           for x, dec in zip(_mm1_each(list(zip(kb_l, k_l)), NT), dec_l)]
    t_l = _tri_inv_each(a_l, ri, ci)
    eg_l = [jnp.exp(gc) for gc in gc_l]
    uu_l = _mm1_each(list(zip(t_l, vb_l)))
    ww_l = _mm1_each([(t, kb * eg) for t, kb, eg in zip(t_l, kb_l, eg_l)])
    qk_l = [jnp.where(ci <= ri, x * dec, 0.0)
            for x, dec in zip(_mm1_each(list(zip(q_l, k_l)), NT), dec_l)]

    for s in range(ROWS // CHUNK):
        r0 = s * CHUNK
        ix = [s * heads + h for h in range(heads)]
        st_l = [s_ref[h] for h in range(heads)]
        ws_l = _mm1_each([(ww_l[i], st) for i, st in zip(ix, st_l)])
        qs_l = _mm1_each([(q_l[i] * eg_l[i], st) for i, st in zip(ix, st_l)])
        vn_l = [uu_l[i] - ws for i, ws in zip(ix, ws_l)]
        qv_l = _mm1_each([(qk_l[i], vn) for i, vn in zip(ix, vn_l)])
        gl_l = [gc_l[i][CHUNK - 1:CHUNK, :] for i in ix]
        kv_l = _mm1_each([(k_l[i] * jnp.exp(gl - gc_l[i]), vn) for i, gl, vn in zip(ix, gl_l, vn_l)], TN)
        for h in range(heads):
            s_ref[h] = st_l[h] * jnp.exp(gl_l[h]) + kv_l[h]
            z = p0_ref[r0:r0 + CHUNK, nconv + h * dv:nconv + (h + 1) * dv]
            o_ref[r0:r0 + CHUNK, h * dv:(h + 1) * dv] = _dn_out_norm(
                qs_l[h] + qv_l[h], z, nw_ref[...]).astype(o_ref.dtype)

    @pl.when(c == pl.num_programs(1) - 1)
    def _():
        s_out_ref[0] = s_ref[...]


def _dn_prompt(p, bat, cw, prow, pcol, nw, batch, nblk, heads, dk, dv):
    m = p.shape[0]
    nconv = 2 * heads * dk + heads * dv
    ba_blk = (2048 + 1792) // LANES
    kern = functools.partial(_dn_prompt_kernel, heads=heads, dk=dk, dv=dv)
    return pl.pallas_call(
        kern,
        grid=(batch, nblk),
        in_specs=[
            pl.BlockSpec((ROWS, 2048), lambda b, c: (b * nblk + c, 0)),
            pl.BlockSpec((ROWS, LANES), lambda b, c: (b * nblk + c, ba_blk)),
            pl.BlockSpec((8, ROWS), lambda b, c: (0, b * nblk + c)),
            pl.BlockSpec((4, nconv), lambda b, c: (0, 0)),
            pl.BlockSpec((2, LANES), lambda b, c: (0, 0)),
            pl.BlockSpec((8, 2), lambda b, c: (0, 0)),
            pl.BlockSpec((1, dv), lambda b, c: (0, 0)),
        ],
        out_specs=[
            pl.BlockSpec((ROWS, heads * dv), lambda b, c: (b * nblk + c, 0)),
            pl.BlockSpec((1, heads, dk, dv), lambda b, c: (b, 0, 0, 0)),
        ],
        out_shape=[jax.ShapeDtypeStruct((m, heads * dv), BF16),
                   jax.ShapeDtypeStruct((batch, heads, dk, dv), F32)],
        scratch_shapes=[pltpu.VMEM((heads, dk, dv), F32), pltpu.VMEM((8 + ROWS, nconv), F32)],
        compiler_params=pltpu.CompilerParams(
            dimension_semantics=("arbitrary", "arbitrary"), vmem_limit_bytes=VMEM_LIMIT),
        name="dn_prompt",
    )(p, p, bat, cw, prow, pcol, nw)


def _col_from_row(row, eye):
    n = row.shape[1]
    return jnp.sum(jnp.where(eye, jnp.broadcast_to(row, (n, n)), 0.0), axis=1, keepdims=True)


def _row_from_col(col, eye):
    n = col.shape[0]
    return jnp.sum(jnp.where(eye, jnp.broadcast_to(col, (n, n)), 0.0), axis=0, keepdims=True)


def _dn_sample_kernel(p0_ref, ba_ref, cache_ref, cw_ref, prow_ref, nw_ref, s_in_ref,
                      o_ref, s_out_ref, q_s, k_s, v_s, g_s, b_s, o_s, *, heads, dk, dv, tb):
    nqk = heads * dk
    nconv = 2 * nqk + heads * dv
    u = p0_ref[:, 0:nconv]
    y = (cw_ref[3:4, :] * u + cw_ref[2:3, :] * cache_ref[2] + cw_ref[1:2, :] * cache_ref[1]
         + cw_ref[0:1, :] * cache_ref[0])
    qkv = _silu(y)
    ba = ba_ref[...]
    b_s[...] = _sigmoid(ba)
    g_s[...] = -jnp.exp(prow_ref[0:1, :]) * _softplus(ba + prow_ref[1:2, :])
    scale = dk ** -0.5
    for h in range(heads):
        q_s[:, h * dk:(h + 1) * dk] = _l2norm_rows(qkv[:, h * dk:(h + 1) * dk]) * scale
        k_s[:, h * dk:(h + 1) * dk] = _l2norm_rows(qkv[:, nqk + h * dk:nqk + (h + 1) * dk])
    v_s[...] = qkv[:, 2 * nqk:]
    ri, ci = _iota2(dk, dk)
    eye = ri == ci

    for i in range(tb):
        for h in range(heads):
            k_row = k_s[i:i + 1, h * dk:(h + 1) * dk]
            q_row = q_s[i:i + 1, h * dk:(h + 1) * dk]
            v_row = v_s[i:i + 1, h * dv:(h + 1) * dv]
            g = g_s[i:i + 1, heads + h:heads + h + 1]
            beta = b_s[i:i + 1, h:h + 1]
            k_col = _col_from_row(k_row, eye)
            q_col = _col_from_row(q_row, eye)
            st = s_in_ref[i, h] * jnp.exp(g)
            kv = jnp.sum(st * k_col, axis=0, keepdims=True)
            delta = (v_row - kv) * beta
            st = st + k_col * delta
            s_out_ref[i, h] = st
            o_s[i:i + 1, h * dv:(h + 1) * dv] = jnp.sum(st * q_col, axis=0, keepdims=True)
    for h in range(heads):
        z = p0_ref[:, nconv + h * dv:nconv + (h + 1) * dv]
        o_ref[:, h * dv:(h + 1) * dv] = _dn_out_norm(
            o_s[:, h * dv:(h + 1) * dv], z, nw_ref[...]).astype(o_ref.dtype)


def _dn_sample(p, cache_t, cw, prow, nw, state, heads, dk, dv, tb=8):
    m = p.shape[0]
    nconv = 2 * heads * dk + heads * dv
    ba_blk = (2048 + 1792) // LANES
    kern = functools.partial(_dn_sample_kernel, heads=heads, dk=dk, dv=dv, tb=tb)
    return pl.pallas_call(
        kern,
        grid=(m // tb,),
        in_specs=[
            pl.BlockSpec((tb, 2048), lambda i: (i, 0)),
            pl.BlockSpec((tb, LANES), lambda i: (i, ba_blk)),
            pl.BlockSpec((3, tb, nconv), lambda i: (0, i, 0)),
            pl.BlockSpec((4, nconv), lambda i: (0, 0)),
            pl.BlockSpec((2, LANES), lambda i: (0, 0)),
            pl.BlockSpec((1, dv), lambda i: (0, 0)),
            pl.BlockSpec((tb, heads, dk, dv), lambda i: (i, 0, 0, 0)),
        ],
        out_specs=[
            pl.BlockSpec((tb, heads * dv), lambda i: (i, 0)),
            pl.BlockSpec((tb, heads, dk, dv), lambda i: (i, 0, 0, 0)),
        ],
        out_shape=[jax.ShapeDtypeStruct((m, heads * dv), BF16),
                   jax.ShapeDtypeStruct(state.shape, F32)],
        scratch_shapes=[pltpu.VMEM((tb, heads * dk), F32), pltpu.VMEM((tb, heads * dk), F32),
                        pltpu.VMEM((tb, heads * dv), F32), pltpu.VMEM((tb, LANES), F32),
                        pltpu.VMEM((tb, LANES), F32), pltpu.VMEM((tb, heads * dv), F32)],
        compiler_params=pltpu.CompilerParams(
            dimension_semantics=("arbitrary",), vmem_limit_bytes=VMEM_LIMIT),
        name="dn_sample",
    )(p, p, cache_t, cw, prow, nw, state)


def _group_sum(x, bd):
    return _mm_mask_rhs(x, bd)


def _rw_prep(prw, prev, par_ref, wwa_ref, g2_ref, bd, nw, rank_w):
    r = prw[:, 0:nw] + (prev[:, 0:nw] - prw[:, 0:nw]) * par_ref[0:1, :]
    kr = prw[:, nw:2 * nw] + (prev[:, nw:2 * nw] - prw[:, nw:2 * nw]) * par_ref[1:2, :]
    vr = prw[:, 2 * nw:3 * nw] + (prev[:, 2 * nw:3 * nw] - prw[:, 2 * nw:3 * nw]) * par_ref[2:3, :]
    c0 = 3 * nw
    lo = prw[:, c0:c0 + 2 * LANES]
    lo = lo + (prev[:, c0:c0 + 2 * LANES] - lo) * par_ref[3:4, 0:2 * LANES]
    wa = lo[:, 0:LANES]
    lane = lax.broadcasted_iota(jnp.int32, wa.shape, 1)
    wa = jnp.where(lane < rank_w, jnp.tanh(wa), wa)
    proj = _mm1(wa, wwa_ref[...])
    w_log = -_softplus(-(par_ref[4:5, :] + proj[:, 0:nw])) - 0.5
    logdec = -jnp.exp(w_log)
    a = _sigmoid(par_ref[5:6, :] + proj[:, nw:2 * nw])
    gate = _mm1(_sigmoid(lo[:, LANES:2 * LANES]), g2_ref[...])
    kk = kr * par_ref[6:7, :]
    kk = kk * lax.rsqrt(_group_sum(kk * kk, bd) + L2_EPS)
    k2 = kr * (1.0 + (a - 1.0) * par_ref[7:8, :])
    return r, logdec, k2, vr, -kk, kk * a, gate


def _rw_epilogue(y, r, k2, v, gate, epi_ref, bd, n):
    mu = _group_sum(y, bd) * (1.0 / n)
    d = y - mu
    var = _group_sum(d * d, bd) * (1.0 / n)
    yn = d * lax.rsqrt(var + GN_EPS) * epi_ref[0:1, :] + epi_ref[1:2, :]
    bonus = _group_sum(r * k2 * epi_ref[2:3, :], bd) * v
    return (yn + bonus) * gate


def _rw_prompt_kernel(p1_ref, par_ref, wwa_ref, g2_ref, bd_ref, epi_ref,
                      o_ref, s_out_ref, s_ref, sbuf_ref, y_s, *, heads, n, rank_w):
    c = pl.program_id(1)
    nw = heads * n
    nrw = 3 * nw + 2 * LANES

    @pl.when(c == 0)
    def _():
        s_ref[...] = jnp.zeros_like(s_ref)
        sbuf_ref[0:8, :] = jnp.zeros((8, nrw), F32)

    prw = p1_ref[:, 0:nrw]
    sbuf_ref[8:8 + ROWS, :] = prw
    prev = sbuf_ref[7:7 + ROWS, :]
    sbuf_ref[0:8, :] = prw[ROWS - 8:ROWS, :]
    bd = bd_ref[...]
    r, logdec, k2, v, a_in, b_in, gate = _rw_prep(prw, prev, par_ref, wwa_ref, g2_ref, bd, nw, rank_w)
    low, _ = _chunk_masks()
    g_all = _mm_mask_lhs(low, logdec)
    ri, ci = _iota2(CHUNK, CHUNK)

    hcs = [(s, h) for s in range(ROWS // CHUNK) for h in range(heads)]
    at_l, rt_l, bt_l, kt_l, bd_l, kd_l, vv_l, egl_l = [], [], [], [], [], [], [], []
    for s, h in hcs:
        sl = (slice(s * CHUNK, (s + 1) * CHUNK), slice(h * n, (h + 1) * n))
        g = g_all[sl]
        eg = jnp.exp(g)
        ieg = jnp.exp(-g)
        gl = g[CHUNK - 1:CHUNK, :]
        tail = jnp.exp(gl - g)
        at_l.append(a_in[sl] * jnp.exp(g - logdec[sl]))
        rt_l.append(r[sl] * eg)
        bt_l.append(b_in[sl] * ieg)
        kt_l.append(k2[sl] * ieg)
        bd_l.append(b_in[sl] * tail)
        kd_l.append(k2[sl] * tail)
        vv_l.append(v[sl])
        egl_l.append(jnp.exp(gl))
    nab_l = [jnp.where(ci < ri, -x, 0.0) for x in _mm1_each(list(zip(at_l, bt_l)), NT)]
    aak_l = [jnp.where(ci < ri, x, 0.0) for x in _mm1_each(list(zip(at_l, kt_l)), NT)]
    rb_l = [jnp.where(ci <= ri, x, 0.0) for x in _mm1_each(list(zip(rt_l, bt_l)), NT)]
    rk_l = [jnp.where(ci <= ri, x, 0.0) for x in _mm1_each(list(zip(rt_l, kt_l)), NT)]
    t_l = _tri_inv_each(nab_l, ri, ci)
    akv_l = _mm1_each(list(zip(aak_l, vv_l)))
    rkv_l = _mm1_each(list(zip(rk_l, vv_l)))
    vkd_l = _mm1_each(list(zip(vv_l, kd_l)), TN)

    for s in range(ROWS // CHUNK):
        r0 = s * CHUNK
        ix = [s * heads + h for h in range(heads)]
        st_l = [s_ref[h] for h in range(heads)]
        as_l = _mm1_each([(at_l[i], st) for i, st in zip(ix, st_l)], NT)
        rs_l = _mm1_each([(rt_l[i], st) for i, st in zip(ix, st_l)], NT)
        uu_l = _mm1_each([(t_l[i], x + akv_l[i]) for i, x in zip(ix, as_l)])
        ru_l = _mm1_each([(rb_l[i], uu) for i, uu in zip(ix, uu_l)])
        ub_l = _mm1_each([(uu, bd_l[i]) for i, uu in zip(ix, uu_l)], TN)
        for h in range(heads):
            i = ix[h]
            y_s[r0:r0 + CHUNK, h * n:(h + 1) * n] = rs_l[h] + ru_l[h] + rkv_l[i]
            s_ref[h] = st_l[h] * egl_l[i] + ub_l[h] + vkd_l[i]

    o_ref[...] = _rw_epilogue(y_s[...], r, k2, v, gate, epi_ref, bd, n).astype(o_ref.dtype)

    @pl.when(c == pl.num_programs(1) - 1)
    def _():
        s_out_ref[0] = s_ref[...]


def _rw_prompt(p, par, wwa, g2, bd, epi, batch, nblk, heads, n, rank_w):
    m = p.shape[0]
    nw = heads * n
    nrw = 3 * nw + 2 * LANES
    kern = functools.partial(_rw_prompt_kernel, heads=heads, n=n, rank_w=rank_w)
    return pl.pallas_call(
        kern,
        grid=(batch, nblk),
        in_specs=[
            pl.BlockSpec((ROWS, 2048), lambda b, c: (b * nblk + c, 1)),
            pl.BlockSpec((8, nw), lambda b, c: (0, 0)),
            pl.BlockSpec((LANES, 2 * nw), lambda b, c: (0, 0)),
            pl.BlockSpec((LANES, nw), lambda b, c: (0, 0)),
            pl.BlockSpec((nw, nw), lambda b, c: (0, 0)),
            pl.BlockSpec((8, nw), lambda b, c: (0, 0)),
        ],
        out_specs=[
            pl.BlockSpec((ROWS, nw), lambda b, c: (b * nblk + c, 0)),
            pl.BlockSpec((1, heads, n, n), lambda b, c: (b, 0, 0, 0)),
        ],
        out_shape=[jax.ShapeDtypeStruct((m, nw), BF16),
                   jax.ShapeDtypeStruct((batch, heads, n, n), F32)],
        scratch_shapes=[pltpu.VMEM((heads, n, n), F32), pltpu.VMEM((8 + ROWS, nrw), F32),
                        pltpu.VMEM((ROWS, nw), F32)],
        compiler_params=pltpu.CompilerParams(
            dimension_semantics=("arbitrary", "arbitrary"), vmem_limit_bytes=VMEM_LIMIT),
        name="rw_prompt",
    )(p, par, wwa, g2, bd, epi)


def _rw_sample_kernel(p1_ref, prev_ref, par_ref, wwa_ref, g2_ref, bd_ref, epi_ref, s_in_ref,
                      o_ref, s_out_ref, r_s, w_s, k_s, v_s, a_s, b_s, y_s, *, heads, n, rank_w, tb):
    nw = heads * n
    nrw = 3 * nw + 2 * LANES
    bd = bd_ref[...]
    prw = p1_ref[:, 0:nrw]
    r, logdec, k2, v, a_in, b_in, gate = _rw_prep(prw, prev_ref[...], par_ref, wwa_ref, g2_ref, bd, nw, rank_w)
    r_s[...] = r
    w_s[...] = jnp.exp(logdec)
    k_s[...] = k2
    v_s[...] = v
    a_s[...] = a_in
    b_s[...] = b_in
    ri, ci = _iota2(n, n)
    eye = ri == ci

    group = 2
    for i0 in range(0, tb, group):
        items = [(i, h) for i in range(i0, i0 + group) for h in range(heads)]
        sls = [(slice(i, i + 1), slice(h * n, (h + 1) * n)) for i, h in items]
        st_l = [s_in_ref[i, h] for i, h in items]
        sa_l = [jnp.sum(st * a_s[sl], axis=1, keepdims=True) for st, sl in zip(st_l, sls)]
        vc_l = [_col_from_row(v_s[sl], eye) for sl in sls]
        st_l = [st * w_s[sl] + sa * b_s[sl] + vc * k_s[sl] for st, sa, vc, sl in zip(st_l, sa_l, vc_l, sls)]
        yc_l = [jnp.sum(st * r_s[sl], axis=1, keepdims=True) for st, sl in zip(st_l, sls)]
        for (i, h), sl, st, yc in zip(items, sls, st_l, yc_l):
            s_out_ref[i, h] = st
            y_s[sl] = _row_from_col(yc, eye)
    o_ref[...] = _rw_epilogue(y_s[...], r, k2, v, gate, epi_ref, bd, n).astype(o_ref.dtype)


def _rw_sample(p, prev, par, wwa, g2, bd, epi, state, heads, n, rank_w, tb=8):
    m = p.shape[0]
    nw = heads * n
    nrw = 3 * nw + 2 * LANES
    kern = functools.partial(_rw_sample_kernel, heads=heads, n=n, rank_w=rank_w, tb=tb)
    return pl.pallas_call(
        kern,
        grid=(m // tb,),
        in_specs=[
            pl.BlockSpec((tb, 2048), lambda i: (i, 1)),
            pl.BlockSpec((tb, nrw), lambda i: (i, 0)),
            pl.BlockSpec((8, nw), lambda i: (0, 0)),
            pl.BlockSpec((LANES, 2 * nw), lambda i: (0, 0)),
            pl.BlockSpec((LANES, nw), lambda i: (0, 0)),
            pl.BlockSpec((nw, nw), lambda i: (0, 0)),
            pl.BlockSpec((8, nw), lambda i: (0, 0)),
            pl.BlockSpec((tb, heads, n, n), lambda i: (i, 0, 0, 0)),
        ],
        out_specs=[
            pl.BlockSpec((tb, nw), lambda i: (i, 0)),
            pl.BlockSpec((tb, heads, n, n), lambda i: (i, 0, 0, 0)),
        ],
        out_shape=[jax.ShapeDtypeStruct((m, nw), BF16), jax.ShapeDtypeStruct(state.shape, F32)],
        scratch_shapes=[pltpu.VMEM((tb, nw), F32)] * 7,
        compiler_params=pltpu.CompilerParams(
            dimension_semantics=("arbitrary",), vmem_limit_bytes=VMEM_LIMIT),
        name="rw_sample",
    )(p, prev, par, wwa, g2, bd, epi, state)


def _merge_kernel(x_ref, odn_ref, orw_ref, gt_ref, wd_ref, wr_ref, wo_ref, h_ref, *, d):
    ga = _sigmoid(gt_ref[:, 0:d])
    gb = _sigmoid(gt_ref[:, d:2 * d])
    m = ga * _dg(odn_ref[...], wd_ref[...]) + gb * _dg(orw_ref[...], wr_ref[...])
    h_ref[...] = x_ref[...] + _dg(m.astype(BF16), wo_ref[...])


def _merge(x, odn, orw, p, wd, wr, wo, batch, nblk_in, nblk_out, skip, tm):
    d = x.shape[1]
    nd = odn.shape[1]
    nr = orw.shape[1]
    kern = functools.partial(_merge_kernel, d=d)
    in_row = lambda b, i: (b * nblk_in + skip + i, 0)
    return pl.pallas_call(
        kern,
        grid=(batch, nblk_out),
        in_specs=[
            pl.BlockSpec((tm, d), in_row),
            pl.BlockSpec((tm, nd), in_row),
            pl.BlockSpec((tm, nr), in_row),
            pl.BlockSpec((tm, 2 * d), lambda b, i: (b * nblk_in + skip + i, 2)),
            pl.BlockSpec((nd, d), lambda b, i: (0, 0)),
            pl.BlockSpec((nr, d), lambda b, i: (0, 0)),
            pl.BlockSpec((d, d), lambda b, i: (0, 0)),
        ],
        out_specs=pl.BlockSpec((tm, d), lambda b, i: (b * nblk_out + i, 0)),
        out_shape=jax.ShapeDtypeStruct((batch * nblk_out * tm, d), F32),
        compiler_params=pltpu.CompilerParams(
            dimension_semantics=("arbitrary", "arbitrary"), vmem_limit_bytes=VMEM_LIMIT),
        name="merge",
    )(x, odn, orw, p, wd, wr, wo)


def _top_values(x, k):
    out = []
    cur = x
    for _ in range(k):
        m = jnp.max(cur, axis=0, keepdims=True)
        out.append(m)
        cur = jnp.where(cur == m, -jnp.inf, cur)
    return out


def _top_values_ranked(x, k):
    out = []
    cur = x
    rank = jnp.full(x.shape, float(k), F32)
    for r in range(k):
        m = jnp.max(cur, axis=0, keepdims=True)
        out.append(m)
        hit = cur == m
        rank = jnp.where(hit, float(r), rank)
        cur = jnp.where(hit, -jnp.inf, cur)
    return out, rank


def _peer_kernel(h_ref, ln2_ref, wq_ref, keys_ref, u_ref, vt_ref, lnf_ref, y_ref,
                 xn_s, q_s, e0_s, e1_s, cnt_s, rank_s, ht0_s, ht1_s, g0_s, g1_s, acc_s,
                 *, heads, nkeys, eb, ne):
    e = pl.program_id(1)
    k = PEER_TOPK

    @pl.when(e == 0)
    def _():
        x = h_ref[...]
        xn = x * lax.rsqrt(jnp.mean(x * x, -1, keepdims=True) + NORM_EPS) * ln2_ref[...]
        xn_s[...] = xn.astype(BF16)
        q = _dg(xn_s[...], wq_ref[...]).astype(BF16)
        for i in range(2 * heads):
            q_s[i] = q[:, i * nkeys:(i + 1) * nkeys]

        lane_tiles = [slice(lt * LANES, (lt + 1) * LANES) for lt in range(h_ref.shape[0] // LANES)]

        def head_body(hd, carry):
            s0_all = _dg(keys_ref[2 * hd], q_s[2 * hd], NT)
            s1_all = _dg(keys_ref[2 * hd + 1], q_s[2 * hd + 1], NT)
            for lt, ls in enumerate(lane_tiles):
                s0 = s0_all[:, ls]
                s1 = s1_all[:, ls]
                rows0, rank0 = _top_values_ranked(s0, k)
                rows1, rank1 = _top_values_ranked(s1, k)
                a = jnp.concatenate(rows0, axis=0)
                b = jnp.concatenate(rows1, axis=0)
                assert k == 16
                cand = jnp.concatenate(
                    [a[0:1, :] + b, a[1:2, :] + b[0:8, :]] + [a[i:i + 1, :] + b[0:8, :] for i in range(2, 8)]
                    + [a[8:16, :] + b[0:1, :]], axis=0)
                cv = _top_values(cand, k)
                z = jnp.exp(cv[0] - cv[0])
                for j in range(1, k):
                    z = z + jnp.exp(cv[j] - cv[0])
                e0_s[hd, lt] = jnp.exp(s0 - a[0:1, :]) * (1.0 / z)
                e1_s[hd, lt] = jnp.exp(s1 - b[0:1, :]).astype(BF16)
                per_rank = jnp.zeros(a.shape, F32)
                for r in range(k):
                    per_rank = per_rank + jnp.where(a + b[r:r + 1, :] >= cv[k - 1], 1.0, 0.0)
                cnt = jnp.zeros(s0.shape, F32)
                for r in range(k):
                    cnt = jnp.where(rank0 == float(r), per_rank[r:r + 1, :], cnt)
                cnt_s[hd, lt] = cnt
                rank_s[hd, lt] = rank1.astype(BF16)
            return carry

        lax.fori_loop(0, heads, head_body, 0)
        acc_s[...] = jnp.zeros_like(acc_s)
        ht1_s[...] = jnp.zeros_like(ht1_s)
        g0_s[...] = jnp.zeros_like(g0_s)
        g1_s[...] = jnp.zeros_like(g1_s)

    per = eb // nkeys
    assert per % 8 == 0
    i0 = pl.multiple_of(jnp.clip(e - 1, 0, ne - 1) * per, 8)
    nlt = ht0_s.shape[0]
    mxw = min(2, nlt)

    def step(ht_w, ht_r, g_w, g_r):
        def pre_act(p):
            rows = slice(p * mxw * LANES, (p + 1) * mxw * LANES)
            res = _dg(u_ref[...], xn_s[rows, :], NT)
            for j in range(mxw):
                ht_w[p * mxw + j] = res[:, j * LANES:(j + 1) * LANES]

        def accumulate(p):
            cols = slice(p * mxw * LANES, (p + 1) * mxw * LANES)
            gp = jnp.concatenate([g_r[p * mxw + j] for j in range(mxw)], axis=1)
            acc_s[:, cols] += _dg(vt_ref[...], gp)

        def gate(lt):
            cnt_l = [cnt_s[hd, lt, pl.ds(i0, per), :].astype(BF16) for hd in range(heads)]
            e0_l = [e0_s[hd, lt, pl.ds(i0, per), :].astype(BF16) for hd in range(heads)]
            for il in range(per):
                w = jnp.zeros((nkeys, LANES), BF16)
                for hd in range(heads):
                    sel = rank_s[hd, lt] < jnp.broadcast_to(cnt_l[hd][il:il + 1, :], (nkeys, LANES))
                    w = w + jnp.where(sel, e1_s[hd, lt], jnp.zeros((), BF16)) * e0_l[hd][il:il + 1, :]
                g = jax.nn.gelu(ht_r[lt, il * nkeys:(il + 1) * nkeys, :].astype(BF16)) * w
                g_w[lt, il * nkeys:(il + 1) * nkeys, :] = g

        for p in range(nlt // mxw):
            pre_act(p)
            gate(p * mxw)
            accumulate(p)
            for j in range(1, mxw):
                gate(p * mxw + j)

    parity = lax.rem(e, 2)

    @pl.when(parity == 0)
    def _():
        step(ht0_s, ht1_s, g1_s, g0_s)

    @pl.when(parity == 1)
    def _():
        step(ht1_s, ht0_s, g0_s, g1_s)

    @pl.when(e == ne + 1)
    def _():
        out = h_ref[...] + acc_s[...].T
        y_ref[...] = out * lax.rsqrt(jnp.mean(out * out, -1, keepdims=True) + NORM_EPS) * lnf_ref[...]


def _peer(h, ln2, wq, keys, u, vt, lnf, tb, eb=1024):
    m, d = h.shape
    nq = wq.shape[1]
    nk2, nkeys, half = keys.shape
    heads = nk2 // 2
    nexp = u.shape[0]
    ne = nexp // eb
    assert tb % LANES == 0
    nlt = tb // LANES
    kern = functools.partial(_peer_kernel, heads=heads, nkeys=nkeys, eb=eb, ne=ne)
    return pl.pallas_call(
        kern,
        grid=(m // tb, ne + 2),
        in_specs=[
            pl.BlockSpec((tb, d), lambda t, e: (t, 0)),
            pl.BlockSpec((1, d), lambda t, e: (0, 0)),
            pl.BlockSpec((d, nq), lambda t, e: (0, 0)),
            pl.BlockSpec((nk2, nkeys, half), lambda t, e: (0, 0, 0)),
            pl.BlockSpec((eb, d), lambda t, e: (jnp.minimum(e, ne - 1), 0)),
            pl.BlockSpec((d, eb), lambda t, e: (0, jnp.clip(e - 2, 0, ne - 1))),
            pl.BlockSpec((1, d), lambda t, e: (0, 0)),
        ],
        out_specs=pl.BlockSpec((tb, d), lambda t, e: (t, 0)),
        out_shape=jax.ShapeDtypeStruct((m, d), F32),
        scratch_shapes=[
            pltpu.VMEM((tb, d), BF16),
            pltpu.VMEM((nk2, tb, half), BF16),
            pltpu.VMEM((heads, nlt, nkeys, LANES), F32),
            pltpu.VMEM((heads, nlt, nkeys, LANES), BF16),
            pltpu.VMEM((heads, nlt, nkeys, LANES), F32),
            pltpu.VMEM((heads, nlt, nkeys, LANES), BF16),
            pltpu.VMEM((nlt, eb, LANES), F32),
            pltpu.VMEM((nlt, eb, LANES), F32),
            pltpu.VMEM((nlt, eb, LANES), BF16),
            pltpu.VMEM((nlt, eb, LANES), BF16),
            pltpu.VMEM((d, tb), F32),
        ],
        compiler_params=pltpu.CompilerParams(
            dimension_semantics=("arbitrary", "arbitrary"), vmem_limit_bytes=VMEM_LIMIT),
        name="peer",
    )(h, ln2, wq, keys, u, vt, lnf)


def _pad_lanes(v, width, offset=0):
    out = jnp.zeros((width,), F32)
    return out.at[offset:offset + v.shape[0]].set(v.astype(F32))


def kernel(x_prompt, x_sample, cache_dn_conv, state_dn, cache_rw_shift, state_rw, meta_tokens, ln1, w_in, dn_conv_w, dn_a_log, dn_dt_bias, dn_norm_w, rw_mu, rw_w0, rw_w2, rw_a0, rw_a2, rw_g2, rw_k_k, rw_k_a, rw_r_k, rw_gn_w, rw_gn_b, w_up_dn, w_up_rw, w_out, ln2, peer_wq, peer_keys, peer_u, peer_v, ln_f):
    bp, seq, d = x_prompt.shape
    bs, seq_s, _ = x_sample.shape
    depth = w_in.shape[0]
    assert depth == 1 and seq_s == 1
    n_meta = meta_tokens.shape[0]
    _, _, heads_dn, dk, dv = state_dn.shape
    _, _, heads_rw, n_rw, _ = state_rw.shape
    n_conv = dn_conv_w.shape[2]
    nqkv = 2 * heads_dn * dk + heads_dn * dv
    assert n_conv == nqkv == 1536 and dn_conv_w.shape[1] == 4 and heads_dn * dv == 512
    nw = heads_rw * n_rw
    rank_w, rank_a, rank_g = rw_w2.shape[1], rw_a2.shape[1], rw_g2.shape[1]
    assert nw == 512 and rank_w + rank_a == LANES and rank_g == LANES
    n_rwp = 3 * nw + rank_w + rank_a + rank_g
    o_z = nqkv
    o_beta = o_z + heads_dn * dv
    o_alpha = o_beta + heads_dn
    o_rw = o_alpha + heads_dn
    o_gate = o_rw + n_rwp
    assert w_in.shape[2] == o_gate + 2 * d and 2 * heads_dn == 8

    wi = w_in[0]
    seg1 = jnp.concatenate([wi[:, o_rw:o_gate], wi[:, o_beta:o_rw],
                            jnp.zeros((d, 2048 - n_rwp - 2 * heads_dn), F32)], axis=1)
    w_packed = jnp.concatenate([wi[:, 0:o_beta], seg1, wi[:, o_gate:]], axis=1).astype(BF16)
    w_bt = wi[:, o_beta:o_rw].T.astype(BF16)
    g1 = ln1[0][None, :]
    cw = dn_conv_w[0]
    prow = jnp.stack([_pad_lanes(dn_a_log[0], LANES, heads_dn), _pad_lanes(dn_dt_bias[0], LANES, heads_dn)])
    pcol = jnp.stack([_pad_lanes(dn_a_log[0], 8, heads_dn), _pad_lanes(dn_dt_bias[0], 8, heads_dn)], axis=1)
    nwd = dn_norm_w[0][None, :]
    mu = rw_mu[0]
    par = jnp.stack([mu[0:nw], mu[nw:2 * nw], mu[2 * nw:3 * nw], _pad_lanes(mu[3 * nw:], nw),
                     rw_w0[0], rw_a0[0], rw_k_k[0], rw_k_a[0]])
    wwa = jnp.zeros((LANES, 2 * nw), F32)
    wwa = wwa.at[0:rank_w, 0:nw].set(rw_w2[0]).at[rank_w:, nw:].set(rw_a2[0]).astype(BF16)
    g2 = rw_g2[0].astype(BF16)
    grp = jnp.arange(nw) // n_rw
    bd = (grp[:, None] == grp[None, :]).astype(BF16)
    epi = jnp.zeros((8, nw), F32).at[0].set(rw_gn_w[0]).at[1].set(rw_gn_b[0]).at[2].set(rw_r_k[0].reshape(-1))
    wd = w_up_dn[0].astype(BF16)
    wr = w_up_rw[0].astype(BF16)
    wo = w_out[0].astype(BF16)
    l2 = ln2[0][None, :]
    wq = peer_wq[0].astype(BF16)
    pk = peer_keys[0]
    keys = pk.reshape(pk.shape[0] * pk.shape[1], pk.shape[2], pk.shape[3]).astype(BF16)
    pu = peer_u[0].astype(BF16)
    pvt = peer_v[0].T.astype(BF16)
    lf = ln_f[None, :]

    t_real = n_meta + seq
    nblk = -(-t_real // ROWS)
    t_pad = nblk * ROWS
    n_front = t_pad - t_real
    assert (n_front + n_meta) % ROWS == 0 and seq % ROWS == 0
    skip = (n_front + n_meta) // ROWS
    xp = jnp.concatenate([jnp.zeros((bp, n_front, d), F32),
                          jnp.broadcast_to(meta_tokens.astype(F32), (bp, n_meta, d)), x_prompt], axis=1)
    xp = xp.reshape(bp * t_pad, d)
    tm = max(t for t in (1024, 512, 256, ROWS) if (bp * t_pad) % t == 0)
    pp, batp = _norm_proj(xp, g1, w_packed, w_bt, tm)
    odn_p, p_dn = _dn_prompt(pp, batp, cw, prow, pcol, nwd, bp, nblk, heads_dn, dk, dv)
    orw_p, p_rw = _rw_prompt(pp, par, wwa, g2, bd, epi, bp, nblk, heads_rw, n_rw, rank_w)
    hp = _merge(xp, odn_p, orw_p, pp, wd, wr, wo, bp, nblk, seq // ROWS, skip, ROWS)
    y_prompt = _peer(hp, l2, wq, keys, pu, pvt, lf, 512).reshape(bp, seq, d)
    pp3 = pp.reshape(bp, t_pad, -1)
    p_conv = pp3[:, t_pad - 3:, 0:nqkv]
    p_shift = pp3[:, t_pad - 1:, 2048:2048 + n_rwp]

    xs = x_sample.reshape(bs, d)
    ps, _ = _norm_proj(xs, g1, w_packed, w_bt, bs)
    cache_t = jnp.moveaxis(cache_dn_conv[0], 1, 0)
    odn_s, s_dn = _dn_sample(ps, cache_t, cw, prow, nwd, state_dn[0], heads_dn, dk, dv)
    orw_s, s_rw = _rw_sample(ps, cache_rw_shift[0][:, 0, :], par, wwa, g2, bd, epi, state_rw[0],
                             heads_rw, n_rw, rank_w)
    hs = _merge(xs, odn_s, orw_s, ps, wd, wr, wo, 1, 1, 1, 0, bs)
    y_sample = _peer(hs, l2, wq, keys, pu, pvt, lf, bs).reshape(bs, 1, d)
    s_conv = jnp.concatenate([cache_dn_conv[0][:, 1:], ps[:, None, 0:nqkv]], axis=1)
    s_shift = ps[:, None, 2048:2048 + n_rwp]

    return (y_prompt, y_sample, p_conv[None], p_dn[None], p_shift[None], p_rw[None],
            s_conv[None], s_dn[None], s_shift[None], s_rw[None])
```

```python
import functools

import jax
import jax.numpy as jnp
from jax import lax
from jax.experimental import pallas as pl
from jax.experimental.pallas import tpu as pltpu

F32 = jnp.float32
BF16 = jnp.bfloat16

NORM_EPS = 1e-6
L2_EPS = 1e-6
GN_EPS = 64e-5
PEER_TOPK = 16

LANES = 128
CHUNK = 64
ROWS = 2 * CHUNK
INV_BLOCK = 16
VMEM_LIMIT = 52 * 1024 * 1024

NN = ((1,), (0,))
NT = ((1,), (1,))
TN = ((0,), (0,))


def _dg(a, b, dims=NN):
    return lax.dot_general(a, b, (dims, ((), ())), preferred_element_type=F32)


def _mm1(a, b, dims=NN):
    return _dg(a.astype(BF16), b.astype(BF16), dims)


def _hi_lo(x):
    hi = x.astype(BF16)
    lo = (x - hi.astype(F32)).astype(BF16)
    return hi, lo


def _mm3(a, b, dims=NN):
    a1, a2 = _hi_lo(a)
    b1, b2 = _hi_lo(b)
    return _dg(a1, b1, dims) + (_dg(a1, b2, dims) + _dg(a2, b1, dims))


def _split3(x):
    x1 = x.astype(BF16)
    r1 = x - x1.astype(F32)
    x2 = r1.astype(BF16)
    x3 = (r1 - x2.astype(F32)).astype(BF16)
    return x1, x2, x3


def _mm_mask_lhs(m, x, dims=NN):
    x1, x2, x3 = _split3(x)
    m = m.astype(BF16)
    return _dg(m, x1, dims) + (_dg(m, x2, dims) + _dg(m, x3, dims))


def _mm_mask_rhs(x, m, dims=NN):
    x1, x2, x3 = _split3(x)
    m = m.astype(BF16)
    return _dg(x1, m, dims) + (_dg(x2, m, dims) + _dg(x3, m, dims))


def _sigmoid(x):
    return jax.nn.sigmoid(x)


def _silu(x):
    return x * jax.nn.sigmoid(x)


def _softplus(x):
    return jnp.maximum(x, 0.0) + jnp.log(1.0 + jnp.exp(-jnp.abs(x)))


def _iota2(n, m):
    return (lax.broadcasted_iota(jnp.int32, (n, m), 0), lax.broadcasted_iota(jnp.int32, (n, m), 1))


def _mm3_each(pairs, dims=NN):
    split = [(_hi_lo(a), _hi_lo(b)) for a, b in pairs]
    return [_dg(a1, b1, dims) + (_dg(a1, b2, dims) + _dg(a2, b1, dims)) for (a1, a2), (b1, b2) in split]


def _mm1_each(pairs, dims=NN):
    cast = [(a.astype(BF16), b.astype(BF16)) for a, b in pairs]
    return [_dg(a, b, dims) for a, b in cast]


def _tri_inv_each(mats, ri, ci):
    assert mats[0].shape == (CHUNK, CHUNK) and CHUNK == 4 * INV_BLOCK and INV_BLOCK == 16
    eye = jnp.where(ri == ci, 1.0, 0.0).astype(F32)
    same = lax.shift_right_logical(ri, 4) == lax.shift_right_logical(ci, 4)
    n = [jnp.where(same, a, 0.0) for a in mats]
    e = [a - x for a, x in zip(mats, n)]
    n2 = _mm3_each(list(zip(n, n)))
    n4 = _mm3_each(list(zip(n2, n2)))
    n8 = _mm3_each(list(zip(n4, n4)))
    d = _mm3_each([(eye - x, eye + y) for x, y in zip(n, n2)])
    d = _mm3_each([(x, eye + y) for x, y in zip(d, n4)])
    d = _mm3_each([(x, eye + y) for x, y in zip(d, n8)])
    f = _mm3_each(list(zip(d, e)))
    f2 = _mm3_each(list(zip(f, f)))
    x = _mm3_each([(eye - p, eye + q) for p, q in zip(f, f2)])
    return _mm3_each(list(zip(x, d)))


def _chunk_masks():
    ri, ci = _iota2(ROWS, ROWS)
    same = lax.shift_right_logical(ri, 6) == lax.shift_right_logical(ci, 6)
    low = jnp.where(same, jnp.where(ci <= ri, 1.0, 0.0), 0.0).astype(BF16)
    up = jnp.where(same, jnp.where(ri <= ci, 1.0, 0.0), 0.0).astype(BF16)
    return low, up


def _norm_proj_kernel(x_ref, g_ref, w_ref, wbt_ref, o_ref, bt_ref, xn_ref):
    @pl.when(pl.program_id(1) == 0)
    def _():
        x = x_ref[...]
        xn = x * lax.rsqrt(jnp.mean(x * x, -1, keepdims=True) + NORM_EPS) * g_ref[...]
        xn_ref[...] = xn.astype(BF16)
        bt_ref[...] = _dg(wbt_ref[...], xn_ref[...], NT)
    o_ref[...] = _dg(xn_ref[...], w_ref[...])


def _norm_proj(x, g, w_packed, w_bt, tm):
    m, d = x.shape
    n = w_packed.shape[1]
    tn = 2048
    return pl.pallas_call(
        _norm_proj_kernel,
        grid=(m // tm, n // tn),
        in_specs=[
            pl.BlockSpec((tm, d), lambda i, j: (i, 0)),
            pl.BlockSpec((1, d), lambda i, j: (0, 0)),
            pl.BlockSpec((d, tn), lambda i, j: (0, j)),
            pl.BlockSpec((8, d), lambda i, j: (0, 0)),
        ],
        out_specs=[
            pl.BlockSpec((tm, tn), lambda i, j: (i, j)),
            pl.BlockSpec((8, tm), lambda i, j: (0, i)),
        ],
        out_shape=[jax.ShapeDtypeStruct((m, n), F32), jax.ShapeDtypeStruct((8, m), F32)],
        scratch_shapes=[pltpu.VMEM((tm, d), BF16)],
        compiler_params=pltpu.CompilerParams(
            dimension_semantics=("arbitrary", "arbitrary"), vmem_limit_bytes=VMEM_LIMIT),
        name="norm_proj",
    )(x, g, w_packed, w_bt)


def _l2norm_rows(x):
    return x * lax.rsqrt(jnp.sum(x * x, -1, keepdims=True) + L2_EPS)


def _dn_out_norm(o, z, nw):
    on = o * lax.rsqrt(jnp.mean(o * o, -1, keepdims=True) + NORM_EPS) * nw
    return on * _silu(z)


def _dn_prompt_kernel(p0_ref, ba_ref, bat_ref, cw_ref, prow_ref, pcol_ref, nw_ref,
                      o_ref, s_out_ref, s_ref, cbuf_ref, *, heads, dk, dv):
    c = pl.program_id(1)
    nqk = heads * dk
    nconv = 2 * nqk + heads * dv

    @pl.when(c == 0)
    def _():
        s_ref[...] = jnp.zeros_like(s_ref)
        cbuf_ref[0:8, :] = jnp.zeros((8, nconv), F32)

    u = p0_ref[:, 0:nconv]
    cbuf_ref[8:8 + ROWS, :] = u
    y = (cw_ref[3:4, :] * u + cw_ref[2:3, :] * cbuf_ref[7:7 + ROWS, :]
         + cw_ref[1:2, :] * cbuf_ref[6:6 + ROWS, :] + cw_ref[0:1, :] * cbuf_ref[5:5 + ROWS, :])
    cbuf_ref[0:8, :] = u[ROWS - 8:ROWS, :]
    qkv = _silu(y)

    ba = ba_ref[...]
    beta_all = _sigmoid(ba)
    g_col = -jnp.exp(prow_ref[0:1, :]) * _softplus(ba + prow_ref[1:2, :])
    bat = bat_ref[...]
    g_row = -jnp.exp(pcol_ref[:, 0:1]) * _softplus(bat + pcol_ref[:, 1:2])
    low, up = _chunk_masks()
    gc_all = _mm_mask_lhs(low, g_col)
    gr_all = _mm_mask_rhs(g_row, up)
    ri, ci = _iota2(CHUNK, CHUNK)
    scale = dk ** -0.5

    hcs = [(s, h) for s in range(ROWS // CHUNK) for h in range(heads)]
    q_l, k_l, kb_l, vb_l, gc_l, dec_l = [], [], [], [], [], []
    for s, h in hcs:
        r0 = s * CHUNK
        q_l.append(_l2norm_rows(qkv[r0:r0 + CHUNK, h * dk:(h + 1) * dk]) * scale)
        k = _l2norm_rows(qkv[r0:r0 + CHUNK, nqk + h * dk:nqk + (h + 1) * dk])
        beta = beta_all[r0:r0 + CHUNK, h:h + 1]
        gc = gc_all[r0:r0 + CHUNK, heads + h:heads + h + 1]
        gr = gr_all[heads + h:heads + h + 1, r0:r0 + CHUNK]
        k_l.append(k)
        kb_l.append(k * beta)
        vb_l.append(qkv[r0:r0 + CHUNK, 2 * nqk + h * dv:2 * nqk + (h + 1) * dv] * beta)
        gc_l.append(gc)
        dec_l.append(jnp.where(ci <= ri, jnp.exp(jnp.minimum(gc - gr, 0.0)), 0.0))
    a_l = [jnp.where(ci < ri, x * dec, 0.0)
           for x, dec in zip(_mm1_each(list(zip(kb_l, k_l)), NT), dec_l)]
    t_l = _tri_inv_each(a_l, ri, ci)
    eg_l = [jnp.exp(gc) for gc in gc_l]
    uu_l = _mm1_each(list(zip(t_l, vb_l)))
    ww_l = _mm1_each([(t, kb * eg) for t, kb, eg in zip(t_l, kb_l, eg_l)])
    qk_l = [jnp.where(ci <= ri, x * dec, 0.0)
            for x, dec in zip(_mm1_each(list(zip(q_l, k_l)), NT), dec_l)]

    for s in range(ROWS // CHUNK):
        r0 = s * CHUNK
        ix = [s * heads + h for h in range(heads)]
        st_l = [s_ref[h] for h in range(heads)]
        ws_l = _mm1_each([(ww_l[i], st) for i, st in zip(ix, st_l)])
        qs_l = _mm1_each([(q_l[i] * eg_l[i], st) for i, st in zip(ix, st_l)])
        vn_l = [uu_l[i] - ws for i, ws in zip(ix, ws_l)]
        qv_l = _mm1_each([(qk_l[i], vn) for i, vn in zip(ix, vn_l)])
        gl_l = [gc_l[i][CHUNK - 1:CHUNK, :] for i in ix]
        kv_l = _mm1_each([(k_l[i] * jnp.exp(gl - gc_l[i]), vn) for i, gl, vn in zip(ix, gl_l, vn_l)], TN)
        for h in range(heads):
            s_ref[h] = st_l[h] * jnp.exp(gl_l[h]) + kv_l[h]
            z = p0_ref[r0:r0 + CHUNK, nconv + h * dv:nconv + (h + 1) * dv]
            o_ref[r0:r0 + CHUNK, h * dv:(h + 1) * dv] = _dn_out_norm(
                qs_l[h] + qv_l[h], z, nw_ref[...]).astype(o_ref.dtype)

    @pl.when(c == pl.num_programs(1) - 1)
    def _():
        s_out_ref[0] = s_ref[...]


def _dn_prompt(p, bat, cw, prow, pcol, nw, batch, nblk, heads, dk, dv):
    m = p.shape[0]
    nconv = 2 * heads * dk + heads * dv
    ba_blk = (2048 + 1792) // LANES
    kern = functools.partial(_dn_prompt_kernel, heads=heads, dk=dk, dv=dv)
    return pl.pallas_call(
        kern,
        grid=(batch, nblk),
        in_specs=[
            pl.BlockSpec((ROWS, 2048), lambda b, c: (b * nblk + c, 0)),
            pl.BlockSpec((ROWS, LANES), lambda b, c: (b * nblk + c, ba_blk)),
            pl.BlockSpec((8, ROWS), lambda b, c: (0, b * nblk + c)),
            pl.BlockSpec((4, nconv), lambda b, c: (0, 0)),
            pl.BlockSpec((2, LANES), lambda b, c: (0, 0)),
            pl.BlockSpec((8, 2), lambda b, c: (0, 0)),
            pl.BlockSpec((1, dv), lambda b, c: (0, 0)),
        ],
        out_specs=[
            pl.BlockSpec((ROWS, heads * dv), lambda b, c: (b * nblk + c, 0)),
            pl.BlockSpec((1, heads, dk, dv), lambda b, c: (b, 0, 0, 0)),
        ],
        out_shape=[jax.ShapeDtypeStruct((m, heads * dv), BF16),
                   jax.ShapeDtypeStruct((batch, heads, dk, dv), F32)],
        scratch_shapes=[pltpu.VMEM((heads, dk, dv), F32), pltpu.VMEM((8 + ROWS, nconv), F32)],
        compiler_params=pltpu.CompilerParams(
            dimension_semantics=("arbitrary", "arbitrary"), vmem_limit_bytes=VMEM_LIMIT),
        name="dn_prompt",
    )(p, p, bat, cw, prow, pcol, nw)


def _col_from_row(row, eye):
    n = row.shape[1]
    return jnp.sum(jnp.where(eye, jnp.broadcast_to(row, (n, n)), 0.0), axis=1, keepdims=True)


def _row_from_col(col, eye):
    n = col.shape[0]
    return jnp.sum(jnp.where(eye, jnp.broadcast_to(col, (n, n)), 0.0), axis=0, keepdims=True)


def _dn_sample_kernel(p0_ref, ba_ref, cache_ref, cw_ref, prow_ref, nw_ref, s_in_ref,
                      o_ref, s_out_ref, q_s, k_s, v_s, g_s, b_s, o_s, *, heads, dk, dv, tb):
    nqk = heads * dk
    nconv = 2 * nqk + heads * dv
    u = p0_ref[:, 0:nconv]
    y = (cw_ref[3:4, :] * u + cw_ref[2:3, :] * cache_ref[2] + cw_ref[1:2, :] * cache_ref[1]
         + cw_ref[0:1, :] * cache_ref[0])
    qkv = _silu(y)
    ba = ba_ref[...]
    b_s[...] = _sigmoid(ba)
    g_s[...] = -jnp.exp(prow_ref[0:1, :]) * _softplus(ba + prow_ref[1:2, :])
    scale = dk ** -0.5
    for h in range(heads):
        q_s[:, h * dk:(h + 1) * dk] = _l2norm_rows(qkv[:, h * dk:(h + 1) * dk]) * scale
        k_s[:, h * dk:(h + 1) * dk] = _l2norm_rows(qkv[:, nqk + h * dk:nqk + (h + 1) * dk])
    v_s[...] = qkv[:, 2 * nqk:]
    ri, ci = _iota2(dk, dk)
    eye = ri == ci

    for i in range(tb):
        for h in range(heads):
            k_row = k_s[i:i + 1, h * dk:(h + 1) * dk]
            q_row = q_s[i:i + 1, h * dk:(h + 1) * dk]
            v_row = v_s[i:i + 1, h * dv:(h + 1) * dv]
            g = g_s[i:i + 1, heads + h:heads + h + 1]
            beta = b_s[i:i + 1, h:h + 1]
            k_col = _col_from_row(k_row, eye)
            q_col = _col_from_row(q_row, eye)
            st = s_in_ref[i, h] * jnp.exp(g)
            kv = jnp.sum(st * k_col, axis=0, keepdims=True)
            delta = (v_row - kv) * beta
            st = st + k_col * delta
            s_out_ref[i, h] = st
            o_s[i:i + 1, h * dv:(h + 1) * dv] = jnp.sum(st * q_col, axis=0, keepdims=True)
    for h in range(heads):
        z = p0_ref[:, nconv + h * dv:nconv + (h + 1) * dv]
        o_ref[:, h * dv:(h + 1) * dv] = _dn_out_norm(
            o_s[:, h * dv:(h + 1) * dv], z, nw_ref[...]).astype(o_ref.dtype)


def _dn_sample(p, cache_t, cw, prow, nw, state, heads, dk, dv, tb=8):
    m = p.shape[0]
    nconv = 2 * heads * dk + heads * dv
    ba_blk = (2048 + 1792) // LANES
    kern = functools.partial(_dn_sample_kernel, heads=heads, dk=dk, dv=dv, tb=tb)
    return pl.pallas_call(
        kern,
        grid=(m // tb,),
        in_specs=[
            pl.BlockSpec((tb, 2048), lambda i: (i, 0)),
            pl.BlockSpec((tb, LANES), lambda i: (i, ba_blk)),
            pl.BlockSpec((3, tb, nconv), lambda i: (0, i, 0)),
            pl.BlockSpec((4, nconv), lambda i: (0, 0)),
            pl.BlockSpec((2, LANES), lambda i: (0, 0)),
            pl.BlockSpec((1, dv), lambda i: (0, 0)),
            pl.BlockSpec((tb, heads, dk, dv), lambda i: (i, 0, 0, 0)),
        ],
        out_specs=[
            pl.BlockSpec((tb, heads * dv), lambda i: (i, 0)),
            pl.BlockSpec((tb, heads, dk, dv), lambda i: (i, 0, 0, 0)),
        ],
        out_shape=[jax.ShapeDtypeStruct((m, heads * dv), BF16),
                   jax.ShapeDtypeStruct(state.shape, F32)],
        scratch_shapes=[pltpu.VMEM((tb, heads * dk), F32), pltpu.VMEM((tb, heads * dk), F32),
                        pltpu.VMEM((tb, heads * dv), F32), pltpu.VMEM((tb, LANES), F32),
                        pltpu.VMEM((tb, LANES), F32), pltpu.VMEM((tb, heads * dv), F32)],
        compiler_params=pltpu.CompilerParams(
            dimension_semantics=("arbitrary",), vmem_limit_bytes=VMEM_LIMIT),
        name="dn_sample",
    )(p, p, cache_t, cw, prow, nw, state)


def _group_sum(x, bd):
    return _mm_mask_rhs(x, bd)


def _rw_prep(prw, prev, par_ref, wwa_ref, g2_ref, bd, nw, rank_w):
    r = prw[:, 0:nw] + (prev[:, 0:nw] - prw[:, 0:nw]) * par_ref[0:1, :]
    kr = prw[:, nw:2 * nw] + (prev[:, nw:2 * nw] - prw[:, nw:2 * nw]) * par_ref[1:2, :]
    vr = prw[:, 2 * nw:3 * nw] + (prev[:, 2 * nw:3 * nw] - prw[:, 2 * nw:3 * nw]) * par_ref[2:3, :]
    c0 = 3 * nw
    lo = prw[:, c0:c0 + 2 * LANES]
    lo = lo + (prev[:, c0:c0 + 2 * LANES] - lo) * par_ref[3:4, 0:2 * LANES]
    wa = lo[:, 0:LANES]
    lane = lax.broadcasted_iota(jnp.int32, wa.shape, 1)
    wa = jnp.where(lane < rank_w, jnp.tanh(wa), wa)
    proj = _mm1(wa, wwa_ref[...])
    w_log = -_softplus(-(par_ref[4:5, :] + proj[:, 0:nw])) - 0.5
    logdec = -jnp.exp(w_log)
    a = _sigmoid(par_ref[5:6, :] + proj[:, nw:2 * nw])
    gate = _mm1(_sigmoid(lo[:, LANES:2 * LANES]), g2_ref[...])
    kk = kr * par_ref[6:7, :]
    kk = kk * lax.rsqrt(_group_sum(kk * kk, bd) + L2_EPS)
    k2 = kr * (1.0 + (a - 1.0) * par_ref[7:8, :])
    return r, logdec, k2, vr, -kk, kk * a, gate


def _rw_epilogue(y, r, k2, v, gate, epi_ref, bd, n):
    mu = _group_sum(y, bd) * (1.0 / n)
    d = y - mu
    var = _group_sum(d * d, bd) * (1.0 / n)
    yn = d * lax.rsqrt(var + GN_EPS) * epi_ref[0:1, :] + epi_ref[1:2, :]
    bonus = _group_sum(r * k2 * epi_ref[2:3, :], bd) * v
    return (yn + bonus) * gate


def _rw_prompt_kernel(p1_ref, par_ref, wwa_ref, g2_ref, bd_ref, epi_ref,
                      o_ref, s_out_ref, s_ref, sbuf_ref, y_s, *, heads, n, rank_w):
    c = pl.program_id(1)
    nw = heads * n
    nrw = 3 * nw + 2 * LANES

    @pl.when(c == 0)
    def _():
        s_ref[...] = jnp.zeros_like(s_ref)
        sbuf_ref[0:8, :] = jnp.zeros((8, nrw), F32)

    prw = p1_ref[:, 0:nrw]
    sbuf_ref[8:8 + ROWS, :] = prw
    prev = sbuf_ref[7:7 + ROWS, :]
    sbuf_ref[0:8, :] = prw[ROWS - 8:ROWS, :]
    bd = bd_ref[...]
    r, logdec, k2, v, a_in, b_in, gate = _rw_prep(prw, prev, par_ref, wwa_ref, g2_ref, bd, nw, rank_w)
    low, _ = _chunk_masks()
    g_all = _mm_mask_lhs(low, logdec)
    ri, ci = _iota2(CHUNK, CHUNK)

    hcs = [(s, h) for s in range(ROWS // CHUNK) for h in range(heads)]
    at_l, rt_l, bt_l, kt_l, bd_l, kd_l, vv_l, egl_l = [], [], [], [], [], [], [], []
    for s, h in hcs:
        sl = (slice(s * CHUNK, (s + 1) * CHUNK), slice(h * n, (h + 1) * n))
        g = g_all[sl]
        eg = jnp.exp(g)
        ieg = jnp.exp(-g)
        gl = g[CHUNK - 1:CHUNK, :]
        tail = jnp.exp(gl - g)
        at_l.append(a_in[sl] * jnp.exp(g - logdec[sl]))
        rt_l.append(r[sl] * eg)
        bt_l.append(b_in[sl] * ieg)
        kt_l.append(k2[sl] * ieg)
        bd_l.append(b_in[sl] * tail)
        kd_l.append(k2[sl] * tail)
        vv_l.append(v[sl])
        egl_l.append(jnp.exp(gl))
    nab_l = [jnp.where(ci < ri, -x, 0.0) for x in _mm1_each(list(zip(at_l, bt_l)), NT)]
    aak_l = [jnp.where(ci < ri, x, 0.0) for x in _mm1_each(list(zip(at_l, kt_l)), NT)]
    rb_l = [jnp.where(ci <= ri, x, 0.0) for x in _mm1_each(list(zip(rt_l, bt_l)), NT)]
    rk_l = [jnp.where(ci <= ri, x, 0.0) for x in _mm1_each(list(zip(rt_l, kt_l)), NT)]
    t_l = _tri_inv_each(nab_l, ri, ci)
    akv_l = _mm1_each(list(zip(aak_l, vv_l)))
    rkv_l = _mm1_each(list(zip(rk_l, vv_l)))
    vkd_l = _mm1_each(list(zip(vv_l, kd_l)), TN)

    for s in range(ROWS // CHUNK):
        r0 = s * CHUNK
        ix = [s * heads + h for h in range(heads)]
        st_l = [s_ref[h] for h in range(heads)]
        as_l = _mm1_each([(at_l[i], st) for i, st in zip(ix, st_l)], NT)
        rs_l = _mm1_each([(rt_l[i], st) for i, st in zip(ix, st_l)], NT)
        uu_l = _mm1_each([(t_l[i], x + akv_l[i]) for i, x in zip(ix, as_l)])
        ru_l = _mm1_each([(rb_l[i], uu) for i, uu in zip(ix, uu_l)])
        ub_l = _mm1_each([(uu, bd_l[i]) for i, uu in zip(ix, uu_l)], TN)
        for h in range(heads):
            i = ix[h]
            y_s[r0:r0 + CHUNK, h * n:(h + 1) * n] = rs_l[h] + ru_l[h] + rkv_l[i]
            s_ref[h] = st_l[h] * egl_l[i] + ub_l[h] + vkd_l[i]

    o_ref[...] = _rw_epilogue(y_s[...], r, k2, v, gate, epi_ref, bd, n).astype(o_ref.dtype)

    @pl.when(c == pl.num_programs(1) - 1)
    def _():
        s_out_ref[0] = s_ref[...]


def _rw_prompt(p, par, wwa, g2, bd, epi, batch, nblk, heads, n, rank_w):
    m = p.shape[0]
    nw = heads * n
    nrw = 3 * nw + 2 * LANES
    kern = functools.partial(_rw_prompt_kernel, heads=heads, n=n, rank_w=rank_w)
    return pl.pallas_call(
        kern,
        grid=(batch, nblk),
        in_specs=[
            pl.BlockSpec((ROWS, 2048), lambda b, c: (b * nblk + c, 1)),
            pl.BlockSpec((8, nw), lambda b, c: (0, 0)),
            pl.BlockSpec((LANES, 2 * nw), lambda b, c: (0, 0)),
            pl.BlockSpec((LANES, nw), lambda b, c: (0, 0)),
            pl.BlockSpec((nw, nw), lambda b, c: (0, 0)),
            pl.BlockSpec((8, nw), lambda b, c: (0, 0)),
        ],
        out_specs=[
            pl.BlockSpec((ROWS, nw), lambda b, c: (b * nblk + c, 0)),
            pl.BlockSpec((1, heads, n, n), lambda b, c: (b, 0, 0, 0)),
        ],
        out_shape=[jax.ShapeDtypeStruct((m, nw), BF16),
                   jax.ShapeDtypeStruct((batch, heads, n, n), F32)],
        scratch_shapes=[pltpu.VMEM((heads, n, n), F32), pltpu.VMEM((8 + ROWS, nrw), F32),
                        pltpu.VMEM((ROWS, nw), F32)],
        compiler_params=pltpu.CompilerParams(
            dimension_semantics=("arbitrary", "arbitrary"), vmem_limit_bytes=VMEM_LIMIT),
        name="rw_prompt",
    )(p, par, wwa, g2, bd, epi)


def _rw_sample_kernel(p1_ref, prev_ref, par_ref, wwa_ref, g2_ref, bd_ref, epi_ref, s_in_ref,
                      o_ref, s_out_ref, r_s, w_s, k_s, v_s, a_s, b_s, y_s, *, heads, n, rank_w, tb):
    nw = heads * n
    nrw = 3 * nw + 2 * LANES
    bd = bd_ref[...]
    prw = p1_ref[:, 0:nrw]
    r, logdec, k2, v, a_in, b_in, gate = _rw_prep(prw, prev_ref[...], par_ref, wwa_ref, g2_ref, bd, nw, rank_w)
    r_s[...] = r
    w_s[...] = jnp.exp(logdec)
    k_s[...] = k2
    v_s[...] = v
    a_s[...] = a_in
    b_s[...] = b_in
    ri, ci = _iota2(n, n)
    eye = ri == ci

    group = 2
    for i0 in range(0, tb, group):
        items = [(i, h) for i in range(i0, i0 + group) for h in range(heads)]
        sls = [(slice(i, i + 1), slice(h * n, (h + 1) * n)) for i, h in items]
        st_l = [s_in_ref[i, h] for i, h in items]
        sa_l = [jnp.sum(st * a_s[sl], axis=1, keepdims=True) for st, sl in zip(st_l, sls)]
        vc_l = [_col_from_row(v_s[sl], eye) for sl in sls]
        st_l = [st * w_s[sl] + sa * b_s[sl] + vc * k_s[sl] for st, sa, vc, sl in zip(st_l, sa_l, vc_l, sls)]
        yc_l = [jnp.sum(st * r_s[sl], axis=1, keepdims=True) for st, sl in zip(st_l, sls)]
        for (i, h), sl, st, yc in zip(items, sls, st_l, yc_l):
            s_out_ref[i, h] = st
            y_s[sl] = _row_from_col(yc, eye)
    o_ref[...] = _rw_epilogue(y_s[...], r, k2, v, gate, epi_ref, bd, n).astype(o_ref.dtype)


def _rw_sample(p, prev, par, wwa, g2, bd, epi, state, heads, n, rank_w, tb=8):
    m = p.shape[0]
    nw = heads * n
    nrw = 3 * nw + 2 * LANES
    kern = functools.partial(_rw_sample_kernel, heads=heads, n=n, rank_w=rank_w, tb=tb)
    return pl.pallas_call(
        kern,
        grid=(m // tb,),
        in_specs=[
            pl.BlockSpec((tb, 2048), lambda i: (i, 1)),
            pl.BlockSpec((tb, nrw), lambda i: (i, 0)),
            pl.BlockSpec((8, nw), lambda i: (0, 0)),
            pl.BlockSpec((LANES, 2 * nw), lambda i: (0, 0)),
            pl.BlockSpec((LANES, nw), lambda i: (0, 0)),
            pl.BlockSpec((nw, nw), lambda i: (0, 0)),
            pl.BlockSpec((8, nw), lambda i: (0, 0)),
            pl.BlockSpec((tb, heads, n, n), lambda i: (i, 0, 0, 0)),
        ],
        out_specs=[
            pl.BlockSpec((tb, nw), lambda i: (i, 0)),
            pl.BlockSpec((tb, heads, n, n), lambda i: (i, 0, 0, 0)),
        ],
        out_shape=[jax.ShapeDtypeStruct((m, nw), BF16), jax.ShapeDtypeStruct(state.shape, F32)],
        scratch_shapes=[pltpu.VMEM((tb, nw), F32)] * 7,
        compiler_params=pltpu.CompilerParams(
            dimension_semantics=("arbitrary",), vmem_limit_bytes=VMEM_LIMIT),
        name="rw_sample",
    )(p, prev, par, wwa, g2, bd, epi, state)


def _merge_kernel(x_ref, odn_ref, orw_ref, gt_ref, wd_ref, wr_ref, wo_ref, h_ref, *, d):
    ga = _sigmoid(gt_ref[:, 0:d])
    gb = _sigmoid(gt_ref[:, d:2 * d])
    m = ga * _dg(odn_ref[...], wd_ref[...]) + gb * _dg(orw_ref[...], wr_ref[...])
    h_ref[...] = x_ref[...] + _dg(m.astype(BF16), wo_ref[...])


def _merge(x, odn, orw, p, wd, wr, wo, batch, nblk_in, nblk_out, skip, tm):
    d = x.shape[1]
    nd = odn.shape[1]
    nr = orw.shape[1]
    kern = functools.partial(_merge_kernel, d=d)
    in_row = lambda b, i: (b * nblk_in + skip + i, 0)
    return pl.pallas_call(
        kern,
        grid=(batch, nblk_out),
        in_specs=[
            pl.BlockSpec((tm, d), in_row),
            pl.BlockSpec((tm, nd), in_row),
            pl.BlockSpec((tm, nr), in_row),
            pl.BlockSpec((tm, 2 * d), lambda b, i: (b * nblk_in + skip + i, 2)),
            pl.BlockSpec((nd, d), lambda b, i: (0, 0)),
            pl.BlockSpec((nr, d), lambda b, i: (0, 0)),
            pl.BlockSpec((d, d), lambda b, i: (0, 0)),
        ],
        out_specs=pl.BlockSpec((tm, d), lambda b, i: (b * nblk_out + i, 0)),
        out_shape=jax.ShapeDtypeStruct((batch * nblk_out * tm, d), F32),
        compiler_params=pltpu.CompilerParams(
            dimension_semantics=("arbitrary", "arbitrary"), vmem_limit_bytes=VMEM_LIMIT),
        name="merge",
    )(x, odn, orw, p, wd, wr, wo)


def _top_values_each(xs, k, ranked):
    rows = [[] for _ in xs]
    cur = list(xs)
    rank = [jnp.full(x.shape, float(k), F32) if rk else None for x, rk in zip(xs, ranked)]
    for r in range(k):
        top = [jnp.max(c, axis=0, keepdims=True) for c in cur]
        hit = [c == m for c, m in zip(cur, top)]
        for i, m in enumerate(top):
            rows[i].append(m)
            if ranked[i]:
                rank[i] = jnp.where(hit[i], float(r), rank[i])
        cur = [jnp.where(h, -jnp.inf, c) for h, c in zip(hit, cur)]
    return [jnp.concatenate(r, axis=0) for r in rows], rank


def _peer_kernel(h_ref, ln2_ref, wq_ref, keys_ref, u_ref, vt_ref, lnf_ref, y_ref,
                 xn_s, q_s, e0_s, e1_s, tau_s, ht_s, g_s, acc_s, *, heads, nkeys, eb, ne):
    e = pl.program_id(1)
    k = PEER_TOPK

    @pl.when(e == 0)
    def _():
        x = h_ref[...]
        xn = x * lax.rsqrt(jnp.mean(x * x, -1, keepdims=True) + NORM_EPS) * ln2_ref[...]
        xn_s[...] = xn.astype(BF16)
        q = _dg(xn_s[...], wq_ref[...]).astype(BF16)
        for i in range(2 * heads):
            q_s[i] = q[:, i * nkeys:(i + 1) * nkeys]

        lane_tiles = [slice(lt * LANES, (lt + 1) * LANES) for lt in range(h_ref.shape[0] // LANES)]

        def head_body(hd, carry):
            s0_all = _dg(keys_ref[2 * hd], q_s[2 * hd], NT)
            s1_all = _dg(keys_ref[2 * hd + 1], q_s[2 * hd + 1], NT)
            nt = len(lane_tiles)
            s0_l = [s0_all[:, ls] for ls in lane_tiles]
            s1_l = [s1_all[:, ls] for ls in lane_tiles]
            tops, ranks = _top_values_each(s0_l + s1_l, k, [True] * nt + [False] * nt)
            assert k == 16
            cand_l = []
            for a, b in zip(tops[:nt], tops[nt:]):
                cand_l.append(jnp.concatenate(
                    [a[0:1, :] + b, a[1:2, :] + b[0:8, :]] + [a[i:i + 1, :] + b[0:8, :] for i in range(2, 8)]
                    + [a[8:16, :] + b[0:1, :]], axis=0))
            cv_l, _ = _top_values_each(cand_l, k, [False] * nt)
            for lt in range(nt):
                s0, s1, a, b, cv, rank0 = s0_l[lt], s1_l[lt], tops[lt], tops[nt + lt], cv_l[lt], ranks[lt]
                z = jnp.sum(jnp.exp(cv - cv[0:1, :]), axis=0, keepdims=True)
                e0_s[hd, lt] = jnp.exp(s0 - a[0:1, :]) * (1.0 / z)
                e1_s[hd, lt] = jnp.exp(s1 - b[0:1, :])
                eb1 = jnp.exp(b - b[0:1, :])
                tau_r = jnp.full(a.shape, jnp.inf, F32)
                for r in range(k):
                    tau_r = jnp.where(a + b[r:r + 1, :] >= cv[k - 1:k, :], eb1[r:r + 1, :], tau_r)
                tau = jnp.full(s0.shape, jnp.inf, F32)
                for r in range(k):
                    tau = jnp.where(rank0 == float(r), tau_r[r:r + 1, :], tau)
                tau_s[hd, lt] = tau
            return carry

        lax.fori_loop(0, heads, head_body, 0)
        acc_s[...] = jnp.zeros_like(acc_s)

    per = eb // nkeys
    assert per % 8 == 0
    i0 = pl.multiple_of(e * per, 8)
    nlt = ht_s.shape[0]
    mxw = min(2, nlt)

    def pre_act(p):
        rows = slice(p * mxw * LANES, (p + 1) * mxw * LANES)
        res = _dg(u_ref[...], xn_s[rows, :], NT)
        for j in range(mxw):
            ht_s[p * mxw + j] = res[:, j * LANES:(j + 1) * LANES]

    def accumulate(p):
        cols = slice(p * mxw * LANES, (p + 1) * mxw * LANES)
        gp = jnp.concatenate([g_s[p * mxw + j] for j in range(mxw)], axis=1)
        acc_s[:, cols] += _dg(vt_ref[...], gp)

    def gate(lt):
        tau_l = [tau_s[hd, lt, pl.ds(i0, per), :] for hd in range(heads)]
        e0_l = [e0_s[hd, lt, pl.ds(i0, per), :] for hd in range(heads)]
        for il in range(per):
            w = jnp.zeros((nkeys, LANES), F32)
            for hd in range(heads):
                e1 = e1_s[hd, lt]
                w = w + jnp.where(e1 >= tau_l[hd][il:il + 1, :], e1, 0.0) * e0_l[hd][il:il + 1, :]
            g = jax.nn.gelu(ht_s[lt, il * nkeys:(il + 1) * nkeys, :]) * w
            g_s[lt, il * nkeys:(il + 1) * nkeys, :] = g.astype(BF16)

    npieces = nlt // mxw
    for p in range(npieces):
        pre_act(p)
    for lt in range(nlt):
        gate(lt)
    for p in range(npieces):
        accumulate(p)

    @pl.when(e == ne - 1)
    def _():
        out = h_ref[...] + acc_s[...].T
        y_ref[...] = out * lax.rsqrt(jnp.mean(out * out, -1, keepdims=True) + NORM_EPS) * lnf_ref[...]


def _peer(h, ln2, wq, keys, u, vt, lnf, tb, eb=2048):
    m, d = h.shape
    nq = wq.shape[1]
    nk2, nkeys, half = keys.shape
    heads = nk2 // 2
    nexp = u.shape[0]
    ne = nexp // eb
    assert tb % LANES == 0
    nlt = tb // LANES
    kern = functools.partial(_peer_kernel, heads=heads, nkeys=nkeys, eb=eb, ne=ne)
    return pl.pallas_call(
        kern,
        grid=(m // tb, ne),
        in_specs=[
            pl.BlockSpec((tb, d), lambda t, e: (t, 0)),
            pl.BlockSpec((1, d), lambda t, e: (0, 0)),
            pl.BlockSpec((d, nq), lambda t, e: (0, 0)),
            pl.BlockSpec((nk2, nkeys, half), lambda t, e: (0, 0, 0)),
            pl.BlockSpec((eb, d), lambda t, e: (e, 0)),
            pl.BlockSpec((d, eb), lambda t, e: (0, e)),
            pl.BlockSpec((1, d), lambda t, e: (0, 0)),
        ],
        out_specs=pl.BlockSpec((tb, d), lambda t, e: (t, 0)),
        out_shape=jax.ShapeDtypeStruct((m, d), F32),
        scratch_shapes=[
            pltpu.VMEM((tb, d), BF16),
            pltpu.VMEM((nk2, tb, half), BF16),
            pltpu.VMEM((heads, nlt, nkeys, LANES), F32),
            pltpu.VMEM((heads, nlt, nkeys, LANES), F32),
            pltpu.VMEM((heads, nlt, nkeys, LANES), F32),
            pltpu.VMEM((nlt, eb, LANES), F32),
            pltpu.VMEM((nlt, eb, LANES), BF16),
            pltpu.VMEM((d, tb), F32),
        ],
        compiler_params=pltpu.CompilerParams(
            dimension_semantics=("arbitrary", "arbitrary"), vmem_limit_bytes=VMEM_LIMIT),
        name="peer",
    )(h, ln2, wq, keys, u, vt, lnf)


def _pad_lanes(v, width, offset=0):
    out = jnp.zeros((width,), F32)
    return out.at[offset:offset + v.shape[0]].set(v.astype(F32))


def kernel(x_prompt, x_sample, cache_dn_conv, state_dn, cache_rw_shift, state_rw, meta_tokens, ln1, w_in, dn_conv_w, dn_a_log, dn_dt_bias, dn_norm_w, rw_mu, rw_w0, rw_w2, rw_a0, rw_a2, rw_g2, rw_k_k, rw_k_a, rw_r_k, rw_gn_w, rw_gn_b, w_up_dn, w_up_rw, w_out, ln2, peer_wq, peer_keys, peer_u, peer_v, ln_f):
    bp, seq, d = x_prompt.shape
    bs, seq_s, _ = x_sample.shape
    depth = w_in.shape[0]
    assert depth == 1 and seq_s == 1
    n_meta = meta_tokens.shape[0]
    _, _, heads_dn, dk, dv = state_dn.shape
    _, _, heads_rw, n_rw, _ = state_rw.shape
    n_conv = dn_conv_w.shape[2]
    nqkv = 2 * heads_dn * dk + heads_dn * dv
    assert n_conv == nqkv == 1536 and dn_conv_w.shape[1] == 4 and heads_dn * dv == 512
    nw = heads_rw * n_rw
    rank_w, rank_a, rank_g = rw_w2.shape[1], rw_a2.shape[1], rw_g2.shape[1]
    assert nw == 512 and rank_w + rank_a == LANES and rank_g == LANES
    n_rwp = 3 * nw + rank_w + rank_a + rank_g
    o_z = nqkv
    o_beta = o_z + heads_dn * dv
    o_alpha = o_beta + heads_dn
    o_rw = o_alpha + heads_dn
    o_gate = o_rw + n_rwp
    assert w_in.shape[2] == o_gate + 2 * d and 2 * heads_dn == 8

    wi = w_in[0]
    seg1 = jnp.concatenate([wi[:, o_rw:o_gate], wi[:, o_beta:o_rw],
                            jnp.zeros((d, 2048 - n_rwp - 2 * heads_dn), F32)], axis=1)
    w_packed = jnp.concatenate([wi[:, 0:o_beta], seg1, wi[:, o_gate:]], axis=1).astype(BF16)
    w_bt = wi[:, o_beta:o_rw].T.astype(BF16)
    g1 = ln1[0][None, :]
    cw = dn_conv_w[0]
    prow = jnp.stack([_pad_lanes(dn_a_log[0], LANES, heads_dn), _pad_lanes(dn_dt_bias[0], LANES, heads_dn)])
    pcol = jnp.stack([_pad_lanes(dn_a_log[0], 8, heads_dn), _pad_lanes(dn_dt_bias[0], 8, heads_dn)], axis=1)
    nwd = dn_norm_w[0][None, :]
    mu = rw_mu[0]
    par = jnp.stack([mu[0:nw], mu[nw:2 * nw], mu[2 * nw:3 * nw], _pad_lanes(mu[3 * nw:], nw),
                     rw_w0[0], rw_a0[0], rw_k_k[0], rw_k_a[0]])
    wwa = jnp.zeros((LANES, 2 * nw), F32)
    wwa = wwa.at[0:rank_w, 0:nw].set(rw_w2[0]).at[rank_w:, nw:].set(rw_a2[0]).astype(BF16)
    g2 = rw_g2[0].astype(BF16)
    grp = jnp.arange(nw) // n_rw
    bd = (grp[:, None] == grp[None, :]).astype(BF16)
    epi = jnp.zeros((8, nw), F32).at[0].set(rw_gn_w[0]).at[1].set(rw_gn_b[0]).at[2].set(rw_r_k[0].reshape(-1))
    wd = w_up_dn[0].astype(BF16)
    wr = w_up_rw[0].astype(BF16)
    wo = w_out[0].astype(BF16)
    l2 = ln2[0][None, :]
    wq = peer_wq[0].astype(BF16)
    pk = peer_keys[0]
    keys = pk.reshape(pk.shape[0] * pk.shape[1], pk.shape[2], pk.shape[3]).astype(BF16)
    pu = peer_u[0].astype(BF16)
    pvt = peer_v[0].T.astype(BF16)
    lf = ln_f[None, :]

    t_real = n_meta + seq
    nblk = -(-t_real // ROWS)
    t_pad = nblk * ROWS
    n_front = t_pad - t_real
    assert (n_front + n_meta) % ROWS == 0 and seq % ROWS == 0
    skip = (n_front + n_meta) // ROWS
    xp = jnp.concatenate([jnp.zeros((bp, n_front, d), F32),
                          jnp.broadcast_to(meta_tokens.astype(F32), (bp, n_meta, d)), x_prompt], axis=1)
    xp = xp.reshape(bp * t_pad, d)
    tm = max(t for t in (1024, 512, 256, ROWS) if (bp * t_pad) % t == 0)
    pp, batp = _norm_proj(xp, g1, w_packed, w_bt, tm)
    odn_p, p_dn = _dn_prompt(pp, batp, cw, prow, pcol, nwd, bp, nblk, heads_dn, dk, dv)
    orw_p, p_rw = _rw_prompt(pp, par, wwa, g2, bd, epi, bp, nblk, heads_rw, n_rw, rank_w)
    hp = _merge(xp, odn_p, orw_p, pp, wd, wr, wo, bp, nblk, seq // ROWS, skip, ROWS)
    y_prompt = _peer(hp, l2, wq, keys, pu, pvt, lf, 512).reshape(bp, seq, d)
    pp3 = pp.reshape(bp, t_pad, -1)
    p_conv = pp3[:, t_pad - 3:, 0:nqkv]
    p_shift = pp3[:, t_pad - 1:, 2048:2048 + n_rwp]

    xs = x_sample.reshape(bs, d)
    ps, _ = _norm_proj(xs, g1, w_packed, w_bt, bs)
    cache_t = jnp.moveaxis(cache_dn_conv[0], 1, 0)
    odn_s, s_dn = _dn_sample(ps, cache_t, cw, prow, nwd, state_dn[0], heads_dn, dk, dv)
    orw_s, s_rw = _rw_sample(ps, cache_rw_shift[0][:, 0, :], par, wwa, g2, bd, epi, state_rw[0],
                             heads_rw, n_rw, rank_w)
    hs = _merge(xs, odn_s, orw_s, ps, wd, wr, wo, 1, 1, 1, 0, bs)
    y_sample = _peer(hs, l2, wq, keys, pu, pvt, lf, bs).reshape(bs, 1, d)
    s_conv = jnp.concatenate([cache_dn_conv[0][:, 1:], ps[:, None, 0:nqkv]], axis=1)
    s_shift = ps[:, None, 2048:2048 + n_rwp]

    return (y_prompt, y_sample, p_conv[None], p_dn[None], p_shift[None], p_rw[None],
            s_conv[None], s_dn[None], s_shift[None], s_rw[None])
```

```python
import functools

import jax
import jax.numpy as jnp
from jax import lax
from jax.experimental import pallas as pl
from jax.experimental.pallas import tpu as pltpu

F32 = jnp.float32
BF16 = jnp.bfloat16

NORM_EPS = 1e-6
L2_EPS = 1e-6
GN_EPS = 64e-5
PEER_TOPK = 16

LANES = 128
CHUNK = 64
ROWS = 2 * CHUNK
INV_BLOCK = 16
VMEM_LIMIT = 52 * 1024 * 1024

NN = ((1,), (0,))
NT = ((1,), (1,))
TN = ((0,), (0,))


def _dg(a, b, dims=NN):
    return lax.dot_general(a, b, (dims, ((), ())), preferred_element_type=F32)


def _mm1(a, b, dims=NN):
    return _dg(a.astype(BF16), b.astype(BF16), dims)


def _hi_lo(x):
    hi = x.astype(BF16)
    lo = (x - hi.astype(F32)).astype(BF16)
    return hi, lo


def _mm3(a, b, dims=NN):
    a1, a2 = _hi_lo(a)
    b1, b2 = _hi_lo(b)
    return _dg(a1, b1, dims) + (_dg(a1, b2, dims) + _dg(a2, b1, dims))


def _split3(x):
    x1 = x.astype(BF16)
    r1 = x - x1.astype(F32)
    x2 = r1.astype(BF16)
    x3 = (r1 - x2.astype(F32)).astype(BF16)
    return x1, x2, x3


def _mm_mask_lhs(m, x, dims=NN):
    x1, x2, x3 = _split3(x)
    m = m.astype(BF16)
    return _dg(m, x1, dims) + (_dg(m, x2, dims) + _dg(m, x3, dims))


def _mm_mask_rhs(x, m, dims=NN):
    x1, x2, x3 = _split3(x)
    m = m.astype(BF16)
    return _dg(x1, m, dims) + (_dg(x2, m, dims) + _dg(x3, m, dims))


def _sigmoid(x):
    return jax.nn.sigmoid(x)


def _silu(x):
    return x * jax.nn.sigmoid(x)


def _softplus(x):
    return jnp.maximum(x, 0.0) + jnp.log(1.0 + jnp.exp(-jnp.abs(x)))


def _iota2(n, m):
    return (lax.broadcasted_iota(jnp.int32, (n, m), 0), lax.broadcasted_iota(jnp.int32, (n, m), 1))


def _mm3_each(pairs, dims=NN):
    split = [(_hi_lo(a), _hi_lo(b)) for a, b in pairs]
    return [_dg(a1, b1, dims) + (_dg(a1, b2, dims) + _dg(a2, b1, dims)) for (a1, a2), (b1, b2) in split]


def _mm1_each(pairs, dims=NN):
    cast = [(a.astype(BF16), b.astype(BF16)) for a, b in pairs]
    return [_dg(a, b, dims) for a, b in cast]


def _tri_inv_each(mats, ri, ci):
    assert mats[0].shape == (CHUNK, CHUNK) and CHUNK == 4 * INV_BLOCK and INV_BLOCK == 16
    eye = jnp.where(ri == ci, 1.0, 0.0).astype(F32)
    same = lax.shift_right_logical(ri, 4) == lax.shift_right_logical(ci, 4)
    n = [jnp.where(same, a, 0.0) for a in mats]
    e = [a - x for a, x in zip(mats, n)]
    n2 = _mm1_each(list(zip(n, n)))
    n4 = _mm1_each(list(zip(n2, n2)))
    n8 = _mm1_each(list(zip(n4, n4)))
    d = _mm1_each([(eye - x, eye + y) for x, y in zip(n, n2)])
    d = _mm1_each([(x, eye + y) for x, y in zip(d, n4)])
    d = _mm1_each([(x, eye + y) for x, y in zip(d, n8)])
    f = _mm1_each(list(zip(d, e)))
    f2 = _mm1_each(list(zip(f, f)))
    x = _mm1_each([(eye - p, eye + q) for p, q in zip(f, f2)])
    return _mm1_each(list(zip(x, d)))


def _chunk_masks():
    ri, ci = _iota2(ROWS, ROWS)
    same = lax.shift_right_logical(ri, 6) == lax.shift_right_logical(ci, 6)
    low = jnp.where(same, jnp.where(ci <= ri, 1.0, 0.0), 0.0).astype(BF16)
    up = jnp.where(same, jnp.where(ri <= ci, 1.0, 0.0), 0.0).astype(BF16)
    return low, up


def _norm_proj_kernel(x_ref, g_ref, w_ref, wbt_ref, o_ref, bt_ref, xn_ref):
    @pl.when(pl.program_id(1) == 0)
    def _():
        x = x_ref[...]
        xn = x * lax.rsqrt(jnp.mean(x * x, -1, keepdims=True) + NORM_EPS) * g_ref[...]
        xn_ref[...] = xn.astype(BF16)
        bt_ref[...] = _dg(wbt_ref[...], xn_ref[...], NT)
    o_ref[...] = _dg(xn_ref[...], w_ref[...])


def _norm_proj(x, g, w_packed, w_bt, tm):
    m, d = x.shape
    n = w_packed.shape[1]
    tn = 2048
    return pl.pallas_call(
        _norm_proj_kernel,
        grid=(m // tm, n // tn),
        in_specs=[
            pl.BlockSpec((tm, d), lambda i, j: (i, 0)),
            pl.BlockSpec((1, d), lambda i, j: (0, 0)),
            pl.BlockSpec((d, tn), lambda i, j: (0, j)),
            pl.BlockSpec((8, d), lambda i, j: (0, 0)),
        ],
        out_specs=[
            pl.BlockSpec((tm, tn), lambda i, j: (i, j)),
            pl.BlockSpec((8, tm), lambda i, j: (0, i)),
        ],
        out_shape=[jax.ShapeDtypeStruct((m, n), F32), jax.ShapeDtypeStruct((8, m), F32)],
        scratch_shapes=[pltpu.VMEM((tm, d), BF16)],
        compiler_params=pltpu.CompilerParams(
            dimension_semantics=("arbitrary", "arbitrary"), vmem_limit_bytes=VMEM_LIMIT),
        name="norm_proj",
    )(x, g, w_packed, w_bt)


def _l2norm_rows(x):
    return x * lax.rsqrt(jnp.sum(x * x, -1, keepdims=True) + L2_EPS)


def _dn_out_norm(o, z, nw):
    on = o * lax.rsqrt(jnp.mean(o * o, -1, keepdims=True) + NORM_EPS) * nw
    return on * _silu(z)


def _dn_prompt_kernel(p0_ref, ba_ref, bat_ref, cw_ref, prow_ref, pcol_ref, nw_ref,
                      o_ref, s_out_ref, s_ref, cbuf_ref, *, heads, dk, dv):
    c = pl.program_id(1)
    nqk = heads * dk
    nconv = 2 * nqk + heads * dv

    @pl.when(c == 0)
    def _():
        s_ref[...] = jnp.zeros_like(s_ref)
        cbuf_ref[0:8, :] = jnp.zeros((8, nconv), F32)

    u = p0_ref[:, 0:nconv]
    cbuf_ref[8:8 + ROWS, :] = u
    y = (cw_ref[3:4, :] * u + cw_ref[2:3, :] * cbuf_ref[7:7 + ROWS, :]
         + cw_ref[1:2, :] * cbuf_ref[6:6 + ROWS, :] + cw_ref[0:1, :] * cbuf_ref[5:5 + ROWS, :])
    cbuf_ref[0:8, :] = u[ROWS - 8:ROWS, :]
    qkv = _silu(y)

    ba = ba_ref[...]
    beta_all = _sigmoid(ba)
    g_col = -jnp.exp(prow_ref[0:1, :]) * _softplus(ba + prow_ref[1:2, :])
    bat = bat_ref[...]
    g_row = -jnp.exp(pcol_ref[:, 0:1]) * _softplus(bat + pcol_ref[:, 1:2])
    low, up = _chunk_masks()
    gc_all = _mm_mask_lhs(low, g_col)
    gr_all = _mm_mask_rhs(g_row, up)
    ri, ci = _iota2(CHUNK, CHUNK)
    scale = dk ** -0.5

    hcs = [(s, h) for s in range(ROWS // CHUNK) for h in range(heads)]
    q_l, k_l, kb_l, vb_l, gc_l, dec_l = [], [], [], [], [], []
    for s, h in hcs:
        r0 = s * CHUNK
        q_l.append(_l2norm_rows(qkv[r0:r0 + CHUNK, h * dk:(h + 1) * dk]) * scale)
        k = _l2norm_rows(qkv[r0:r0 + CHUNK, nqk + h * dk:nqk + (h + 1) * dk])
        beta = beta_all[r0:r0 + CHUNK, h:h + 1]
        gc = gc_all[r0:r0 + CHUNK, heads + h:heads + h + 1]
        gr = gr_all[heads + h:heads + h + 1, r0:r0 + CHUNK]
        k_l.append(k)
        kb_l.append(k * beta)
        vb_l.append(qkv[r0:r0 + CHUNK, 2 * nqk + h * dv:2 * nqk + (h + 1) * dv] * beta)
        gc_l.append(gc)
        dec_l.append(jnp.where(ci <= ri, jnp.exp(jnp.minimum(gc - gr, 0.0)), 0.0))
    a_l = [jnp.where(ci < ri, x * dec, 0.0)
           for x, dec in zip(_mm1_each(list(zip(kb_l, k_l)), NT), dec_l)]
    t_l = _tri_inv_each(a_l, ri, ci)
    eg_l = [jnp.exp(gc) for gc in gc_l]
    uu_l = _mm1_each(list(zip(t_l, vb_l)))
    ww_l = _mm1_each([(t, kb * eg) for t, kb, eg in zip(t_l, kb_l, eg_l)])
    qk_l = [jnp.where(ci <= ri, x * dec, 0.0)
            for x, dec in zip(_mm1_each(list(zip(q_l, k_l)), NT), dec_l)]

    for s in range(ROWS // CHUNK):
        r0 = s * CHUNK
        ix = [s * heads + h for h in range(heads)]
        st_l = [s_ref[h] for h in range(heads)]
        ws_l = _mm1_each([(ww_l[i], st) for i, st in zip(ix, st_l)])
        qs_l = _mm1_each([(q_l[i] * eg_l[i], st) for i, st in zip(ix, st_l)])
        vn_l = [uu_l[i] - ws for i, ws in zip(ix, ws_l)]
        qv_l = _mm1_each([(qk_l[i], vn) for i, vn in zip(ix, vn_l)])
        gl_l = [gc_l[i][CHUNK - 1:CHUNK, :] for i in ix]
        kv_l = _mm1_each([(k_l[i] * jnp.exp(gl - gc_l[i]), vn) for i, gl, vn in zip(ix, gl_l, vn_l)], TN)
        for h in range(heads):
            s_ref[h] = st_l[h] * jnp.exp(gl_l[h]) + kv_l[h]
            z = p0_ref[r0:r0 + CHUNK, nconv + h * dv:nconv + (h + 1) * dv]
            o_ref[r0:r0 + CHUNK, h * dv:(h + 1) * dv] = _dn_out_norm(
                qs_l[h] + qv_l[h], z, nw_ref[...]).astype(o_ref.dtype)

    @pl.when(c == pl.num_programs(1) - 1)
    def _():
        s_out_ref[0] = s_ref[...]


def _dn_prompt(p, bat, cw, prow, pcol, nw, batch, nblk, heads, dk, dv):
    m = p.shape[0]
    nconv = 2 * heads * dk + heads * dv
    ba_blk = (2048 + 1792) // LANES
    kern = functools.partial(_dn_prompt_kernel, heads=heads, dk=dk, dv=dv)
    return pl.pallas_call(
        kern,
        grid=(batch, nblk),
        in_specs=[
            pl.BlockSpec((ROWS, 2048), lambda b, c: (b * nblk + c, 0)),
            pl.BlockSpec((ROWS, LANES), lambda b, c: (b * nblk + c, ba_blk)),
            pl.BlockSpec((8, ROWS), lambda b, c: (0, b * nblk + c)),
            pl.BlockSpec((4, nconv), lambda b, c: (0, 0)),
            pl.BlockSpec((2, LANES), lambda b, c: (0, 0)),
            pl.BlockSpec((8, 2), lambda b, c: (0, 0)),
            pl.BlockSpec((1, dv), lambda b, c: (0, 0)),
        ],
        out_specs=[
            pl.BlockSpec((ROWS, heads * dv), lambda b, c: (b * nblk + c, 0)),
            pl.BlockSpec((1, heads, dk, dv), lambda b, c: (b, 0, 0, 0)),
        ],
        out_shape=[jax.ShapeDtypeStruct((m, heads * dv), BF16),
                   jax.ShapeDtypeStruct((batch, heads, dk, dv), F32)],
        scratch_shapes=[pltpu.VMEM((heads, dk, dv), F32), pltpu.VMEM((8 + ROWS, nconv), F32)],
        compiler_params=pltpu.CompilerParams(
            dimension_semantics=("arbitrary", "arbitrary"), vmem_limit_bytes=VMEM_LIMIT),
        name="dn_prompt",
    )(p, p, bat, cw, prow, pcol, nw)


def _col_from_row(row, eye):
    n = row.shape[1]
    return jnp.sum(jnp.where(eye, jnp.broadcast_to(row, (n, n)), 0.0), axis=1, keepdims=True)


def _row_from_col(col, eye):
    n = col.shape[0]
    return jnp.sum(jnp.where(eye, jnp.broadcast_to(col, (n, n)), 0.0), axis=0, keepdims=True)


def _dn_sample_kernel(p0_ref, ba_ref, cache_ref, cw_ref, prow_ref, nw_ref, s_in_ref,
                      o_ref, s_out_ref, q_s, k_s, v_s, g_s, b_s, o_s, *, heads, dk, dv, tb):
    nqk = heads * dk
    nconv = 2 * nqk + heads * dv
    u = p0_ref[:, 0:nconv]
    y = (cw_ref[3:4, :] * u + cw_ref[2:3, :] * cache_ref[2] + cw_ref[1:2, :] * cache_ref[1]
         + cw_ref[0:1, :] * cache_ref[0])
    qkv = _silu(y)
    ba = ba_ref[...]
    b_s[...] = _sigmoid(ba)
    g_s[...] = -jnp.exp(prow_ref[0:1, :]) * _softplus(ba + prow_ref[1:2, :])
    scale = dk ** -0.5
    for h in range(heads):
        q_s[:, h * dk:(h + 1) * dk] = _l2norm_rows(qkv[:, h * dk:(h + 1) * dk]) * scale
        k_s[:, h * dk:(h + 1) * dk] = _l2norm_rows(qkv[:, nqk + h * dk:nqk + (h + 1) * dk])
    v_s[...] = qkv[:, 2 * nqk:]
    ri, ci = _iota2(dk, dk)
    eye = ri == ci

    for i in range(tb):
        for h in range(heads):
            k_row = k_s[i:i + 1, h * dk:(h + 1) * dk]
            q_row = q_s[i:i + 1, h * dk:(h + 1) * dk]
            v_row = v_s[i:i + 1, h * dv:(h + 1) * dv]
            g = g_s[i:i + 1, heads + h:heads + h + 1]
            beta = b_s[i:i + 1, h:h + 1]
            k_col = _col_from_row(k_row, eye)
            q_col = _col_from_row(q_row, eye)
            st = s_in_ref[i, h] * jnp.exp(g)
            kv = jnp.sum(st * k_col, axis=0, keepdims=True)
            delta = (v_row - kv) * beta
            st = st + k_col * delta
            s_out_ref[i, h] = st
            o_s[i:i + 1, h * dv:(h + 1) * dv] = jnp.sum(st * q_col, axis=0, keepdims=True)
    for h in range(heads):
        z = p0_ref[:, nconv + h * dv:nconv + (h + 1) * dv]
        o_ref[:, h * dv:(h + 1) * dv] = _dn_out_norm(
            o_s[:, h * dv:(h + 1) * dv], z, nw_ref[...]).astype(o_ref.dtype)


def _dn_sample(p, cache_t, cw, prow, nw, state, heads, dk, dv, tb=8):
    m = p.shape[0]
    nconv = 2 * heads * dk + heads * dv
    ba_blk = (2048 + 1792) // LANES
    kern = functools.partial(_dn_sample_kernel, heads=heads, dk=dk, dv=dv, tb=tb)
    return pl.pallas_call(
        kern,
        grid=(m // tb,),
        in_specs=[
            pl.BlockSpec((tb, 2048), lambda i: (i, 0)),
            pl.BlockSpec((tb, LANES), lambda i: (i, ba_blk)),
            pl.BlockSpec((3, tb, nconv), lambda i: (0, i, 0)),
            pl.BlockSpec((4, nconv), lambda i: (0, 0)),
            pl.BlockSpec((2, LANES), lambda i: (0, 0)),
            pl.BlockSpec((1, dv), lambda i: (0, 0)),
            pl.BlockSpec((tb, heads, dk, dv), lambda i: (i, 0, 0, 0)),
        ],
        out_specs=[
            pl.BlockSpec((tb, heads * dv), lambda i: (i, 0)),
            pl.BlockSpec((tb, heads, dk, dv), lambda i: (i, 0, 0, 0)),
        ],
        out_shape=[jax.ShapeDtypeStruct((m, heads * dv), BF16),
                   jax.ShapeDtypeStruct(state.shape, F32)],
        scratch_shapes=[pltpu.VMEM((tb, heads * dk), F32), pltpu.VMEM((tb, heads * dk), F32),
                        pltpu.VMEM((tb, heads * dv), F32), pltpu.VMEM((tb, LANES), F32),
                        pltpu.VMEM((tb, LANES), F32), pltpu.VMEM((tb, heads * dv), F32)],
        compiler_params=pltpu.CompilerParams(
            dimension_semantics=("arbitrary",), vmem_limit_bytes=VMEM_LIMIT),
        name="dn_sample",
    )(p, p, cache_t, cw, prow, nw, state)


def _group_sum(x, bd):
    return _mm_mask_rhs(x, bd)


def _rw_prep(prw, prev, par_ref, wwa_ref, g2_ref, bd, nw, rank_w):
    r = prw[:, 0:nw] + (prev[:, 0:nw] - prw[:, 0:nw]) * par_ref[0:1, :]
    kr = prw[:, nw:2 * nw] + (prev[:, nw:2 * nw] - prw[:, nw:2 * nw]) * par_ref[1:2, :]
    vr = prw[:, 2 * nw:3 * nw] + (prev[:, 2 * nw:3 * nw] - prw[:, 2 * nw:3 * nw]) * par_ref[2:3, :]
    c0 = 3 * nw
    lo = prw[:, c0:c0 + 2 * LANES]
    lo = lo + (prev[:, c0:c0 + 2 * LANES] - lo) * par_ref[3:4, 0:2 * LANES]
    wa = lo[:, 0:LANES]
    lane = lax.broadcasted_iota(jnp.int32, wa.shape, 1)
    wa = jnp.where(lane < rank_w, jnp.tanh(wa), wa)
    proj = _mm1(wa, wwa_ref[...])
    w_log = -_softplus(-(par_ref[4:5, :] + proj[:, 0:nw])) - 0.5
    logdec = -jnp.exp(w_log)
    a = _sigmoid(par_ref[5:6, :] + proj[:, nw:2 * nw])
    gate = _mm1(_sigmoid(lo[:, LANES:2 * LANES]), g2_ref[...])
    kk = kr * par_ref[6:7, :]
    kk = kk * lax.rsqrt(_group_sum(kk * kk, bd) + L2_EPS)
    k2 = kr * (1.0 + (a - 1.0) * par_ref[7:8, :])
    return r, logdec, k2, vr, -kk, kk * a, gate


def _rw_epilogue(y, r, k2, v, gate, epi_ref, bd, n):
    mu = _group_sum(y, bd) * (1.0 / n)
    d = y - mu
    var = _group_sum(d * d, bd) * (1.0 / n)
    yn = d * lax.rsqrt(var + GN_EPS) * epi_ref[0:1, :] + epi_ref[1:2, :]
    bonus = _group_sum(r * k2 * epi_ref[2:3, :], bd) * v
    return (yn + bonus) * gate


def _rw_prompt_kernel(p1_ref, par_ref, wwa_ref, g2_ref, bd_ref, epi_ref,
                      o_ref, s_out_ref, s_ref, sbuf_ref, y_s, *, heads, n, rank_w):
    c = pl.program_id(1)
    nw = heads * n
    nrw = 3 * nw + 2 * LANES

    @pl.when(c == 0)
    def _():
        s_ref[...] = jnp.zeros_like(s_ref)
        sbuf_ref[0:8, :] = jnp.zeros((8, nrw), F32)

    prw = p1_ref[:, 0:nrw]
    sbuf_ref[8:8 + ROWS, :] = prw
    prev = sbuf_ref[7:7 + ROWS, :]
    sbuf_ref[0:8, :] = prw[ROWS - 8:ROWS, :]
    bd = bd_ref[...]
    r, logdec, k2, v, a_in, b_in, gate = _rw_prep(prw, prev, par_ref, wwa_ref, g2_ref, bd, nw, rank_w)
    low, _ = _chunk_masks()
    g_all = _mm_mask_lhs(low, logdec)
    ri, ci = _iota2(CHUNK, CHUNK)

    hcs = [(s, h) for s in range(ROWS // CHUNK) for h in range(heads)]
    at_l, rt_l, bt_l, kt_l, bd_l, kd_l, vv_l, egl_l = [], [], [], [], [], [], [], []
    for s, h in hcs:
        sl = (slice(s * CHUNK, (s + 1) * CHUNK), slice(h * n, (h + 1) * n))
        g = g_all[sl]
        eg = jnp.exp(g)
        ieg = jnp.exp(-g)
        gl = g[CHUNK - 1:CHUNK, :]
        tail = jnp.exp(gl - g)
        at_l.append(a_in[sl] * jnp.exp(g - logdec[sl]))
        rt_l.append(r[sl] * eg)
        bt_l.append(b_in[sl] * ieg)
        kt_l.append(k2[sl] * ieg)
        bd_l.append(b_in[sl] * tail)
        kd_l.append(k2[sl] * tail)
        vv_l.append(v[sl])
        egl_l.append(jnp.exp(gl))
    nab_l = [jnp.where(ci < ri, -x, 0.0) for x in _mm1_each(list(zip(at_l, bt_l)), NT)]
    aak_l = [jnp.where(ci < ri, x, 0.0) for x in _mm1_each(list(zip(at_l, kt_l)), NT)]
    rb_l = [jnp.where(ci <= ri, x, 0.0) for x in _mm1_each(list(zip(rt_l, bt_l)), NT)]
    rk_l = [jnp.where(ci <= ri, x, 0.0) for x in _mm1_each(list(zip(rt_l, kt_l)), NT)]
    t_l = _tri_inv_each(nab_l, ri, ci)
    akv_l = _mm1_each(list(zip(aak_l, vv_l)))
    rkv_l = _mm1_each(list(zip(rk_l, vv_l)))
    vkd_l = _mm1_each(list(zip(vv_l, kd_l)), TN)

    for s in range(ROWS // CHUNK):
        r0 = s * CHUNK
        ix = [s * heads + h for h in range(heads)]
        st_l = [s_ref[h] for h in range(heads)]
        as_l = _mm1_each([(at_l[i], st) for i, st in zip(ix, st_l)], NT)
        rs_l = _mm1_each([(rt_l[i], st) for i, st in zip(ix, st_l)], NT)
        uu_l = _mm1_each([(t_l[i], x + akv_l[i]) for i, x in zip(ix, as_l)])
        ru_l = _mm1_each([(rb_l[i], uu) for i, uu in zip(ix, uu_l)])
        ub_l = _mm1_each([(uu, bd_l[i]) for i, uu in zip(ix, uu_l)], TN)
        for h in range(heads):
            i = ix[h]
            y_s[r0:r0 + CHUNK, h * n:(h + 1) * n] = rs_l[h] + ru_l[h] + rkv_l[i]
            s_ref[h] = st_l[h] * egl_l[i] + ub_l[h] + vkd_l[i]

    o_ref[...] = _rw_epilogue(y_s[...], r, k2, v, gate, epi_ref, bd, n).astype(o_ref.dtype)

    @pl.when(c == pl.num_programs(1) - 1)
    def _():
        s_out_ref[0] = s_ref[...]


def _rw_prompt(p, par, wwa, g2, bd, epi, batch, nblk, heads, n, rank_w):
    m = p.shape[0]
    nw = heads * n
    nrw = 3 * nw + 2 * LANES
    kern = functools.partial(_rw_prompt_kernel, heads=heads, n=n, rank_w=rank_w)
    return pl.pallas_call(
        kern,
        grid=(batch, nblk),
        in_specs=[
            pl.BlockSpec((ROWS, 2048), lambda b, c: (b * nblk + c, 1)),
            pl.BlockSpec((8, nw), lambda b, c: (0, 0)),
            pl.BlockSpec((LANES, 2 * nw), lambda b, c: (0, 0)),
            pl.BlockSpec((LANES, nw), lambda b, c: (0, 0)),
            pl.BlockSpec((nw, nw), lambda b, c: (0, 0)),
            pl.BlockSpec((8, nw), lambda b, c: (0, 0)),
        ],
        out_specs=[
            pl.BlockSpec((ROWS, nw), lambda b, c: (b * nblk + c, 0)),
            pl.BlockSpec((1, heads, n, n), lambda b, c: (b, 0, 0, 0)),
        ],
        out_shape=[jax.ShapeDtypeStruct((m, nw), BF16),
                   jax.ShapeDtypeStruct((batch, heads, n, n), F32)],
        scratch_shapes=[pltpu.VMEM((heads, n, n), F32), pltpu.VMEM((8 + ROWS, nrw), F32),
                        pltpu.VMEM((ROWS, nw), F32)],
        compiler_params=pltpu.CompilerParams(
            dimension_semantics=("arbitrary", "arbitrary"), vmem_limit_bytes=VMEM_LIMIT),
        name="rw_prompt",
    )(p, par, wwa, g2, bd, epi)


def _rw_sample_kernel(p1_ref, prev_ref, par_ref, wwa_ref, g2_ref, bd_ref, epi_ref, s_in_ref,
                      o_ref, s_out_ref, r_s, w_s, k_s, v_s, a_s, b_s, y_s, *, heads, n, rank_w, tb):
    nw = heads * n
    nrw = 3 * nw + 2 * LANES
    bd = bd_ref[...]
    prw = p1_ref[:, 0:nrw]
    r, logdec, k2, v, a_in, b_in, gate = _rw_prep(prw, prev_ref[...], par_ref, wwa_ref, g2_ref, bd, nw, rank_w)
    r_s[...] = r
    w_s[...] = jnp.exp(logdec)
    k_s[...] = k2
    v_s[...] = v
    a_s[...] = a_in
    b_s[...] = b_in
    ri, ci = _iota2(n, n)
    eye = ri == ci

    group = 8
    for i0 in range(0, tb, group):
        items = [(i, h) for i in range(i0, i0 + group) for h in range(heads)]
        sls = [(slice(i, i + 1), slice(h * n, (h + 1) * n)) for i, h in items]
        st_l = [s_in_ref[i, h] for i, h in items]
        sa_l = [jnp.sum(st * a_s[sl], axis=1, keepdims=True) for st, sl in zip(st_l, sls)]
        vc_l = [_col_from_row(v_s[sl], eye) for sl in sls]
        st_l = [st * w_s[sl] + sa * b_s[sl] + vc * k_s[sl] for st, sa, vc, sl in zip(st_l, sa_l, vc_l, sls)]
        yc_l = [jnp.sum(st * r_s[sl], axis=1, keepdims=True) for st, sl in zip(st_l, sls)]
        for (i, h), sl, st, yc in zip(items, sls, st_l, yc_l):
            s_out_ref[i, h] = st
            y_s[sl] = _row_from_col(yc, eye)
    o_ref[...] = _rw_epilogue(y_s[...], r, k2, v, gate, epi_ref, bd, n).astype(o_ref.dtype)


def _rw_sample(p, prev, par, wwa, g2, bd, epi, state, heads, n, rank_w, tb=8):
    m = p.shape[0]
    nw = heads * n
    nrw = 3 * nw + 2 * LANES
    kern = functools.partial(_rw_sample_kernel, heads=heads, n=n, rank_w=rank_w, tb=tb)
    return pl.pallas_call(
        kern,
        grid=(m // tb,),
        in_specs=[
            pl.BlockSpec((tb, 2048), lambda i: (i, 1)),
            pl.BlockSpec((tb, nrw), lambda i: (i, 0)),
            pl.BlockSpec((8, nw), lambda i: (0, 0)),
            pl.BlockSpec((LANES, 2 * nw), lambda i: (0, 0)),
            pl.BlockSpec((LANES, nw), lambda i: (0, 0)),
            pl.BlockSpec((nw, nw), lambda i: (0, 0)),
            pl.BlockSpec((8, nw), lambda i: (0, 0)),
            pl.BlockSpec((tb, heads, n, n), lambda i: (i, 0, 0, 0)),
        ],
        out_specs=[
            pl.BlockSpec((tb, nw), lambda i: (i, 0)),
            pl.BlockSpec((tb, heads, n, n), lambda i: (i, 0, 0, 0)),
        ],
        out_shape=[jax.ShapeDtypeStruct((m, nw), BF16), jax.ShapeDtypeStruct(state.shape, F32)],
        scratch_shapes=[pltpu.VMEM((tb, nw), F32)] * 7,
        compiler_params=pltpu.CompilerParams(
            dimension_semantics=("arbitrary",), vmem_limit_bytes=VMEM_LIMIT),
        name="rw_sample",
    )(p, prev, par, wwa, g2, bd, epi, state)


def _merge_kernel(x_ref, odn_ref, orw_ref, gt_ref, wd_ref, wr_ref, wo_ref, h_ref, *, d):
    ga = _sigmoid(gt_ref[:, 0:d])
    gb = _sigmoid(gt_ref[:, d:2 * d])
    m = ga * _dg(odn_ref[...], wd_ref[...]) + gb * _dg(orw_ref[...], wr_ref[...])
    h_ref[...] = x_ref[...] + _dg(m.astype(BF16), wo_ref[...])


def _merge(x, odn, orw, p, wd, wr, wo, batch, nblk_in, nblk_out, skip, tm):
    d = x.shape[1]
    nd = odn.shape[1]
    nr = orw.shape[1]
    kern = functools.partial(_merge_kernel, d=d)
    in_row = lambda b, i: (b * nblk_in + skip + i, 0)
    return pl.pallas_call(
        kern,
        grid=(batch, nblk_out),
        in_specs=[
            pl.BlockSpec((tm, d), in_row),
            pl.BlockSpec((tm, nd), in_row),
            pl.BlockSpec((tm, nr), in_row),
            pl.BlockSpec((tm, 2 * d), lambda b, i: (b * nblk_in + skip + i, 2)),
            pl.BlockSpec((nd, d), lambda b, i: (0, 0)),
            pl.BlockSpec((nr, d), lambda b, i: (0, 0)),
            pl.BlockSpec((d, d), lambda b, i: (0, 0)),
        ],
        out_specs=pl.BlockSpec((tm, d), lambda b, i: (b * nblk_out + i, 0)),
        out_shape=jax.ShapeDtypeStruct((batch * nblk_out * tm, d), F32),
        compiler_params=pltpu.CompilerParams(
            dimension_semantics=("arbitrary", "arbitrary"), vmem_limit_bytes=VMEM_LIMIT),
        name="merge",
    )(x, odn, orw, p, wd, wr, wo)


def _top_values_each(xs, k, ranked):
    rows = [[] for _ in xs]
    cur = list(xs)
    rank = [jnp.full(x.shape, float(k), F32) if rk else None for x, rk in zip(xs, ranked)]
    for r in range(k):
        top = [jnp.max(c, axis=0, keepdims=True) for c in cur]
        hit = [c == m for c, m in zip(cur, top)]
        for i, m in enumerate(top):
            rows[i].append(m)
            if ranked[i]:
                rank[i] = jnp.where(hit[i], float(r), rank[i])
        cur = [jnp.where(h, -jnp.inf, c) for h, c in zip(hit, cur)]
    return [jnp.concatenate(r, axis=0) for r in rows], rank


def _peer_kernel(h_ref, ln2_ref, wq_ref, keys_ref, u_ref, vt_ref, lnf_ref, y_ref,
                 xn_s, q_s, e0_s, e1_s, tau_s, ht_s, g_s, acc_s, *, heads, nkeys, eb, ne):
    e = pl.program_id(1)
    k = PEER_TOPK

    @pl.when(e == 0)
    def _():
        x = h_ref[...]
        xn = x * lax.rsqrt(jnp.mean(x * x, -1, keepdims=True) + NORM_EPS) * ln2_ref[...]
        xn_s[...] = xn.astype(BF16)
        q = _dg(xn_s[...], wq_ref[...]).astype(BF16)
        for i in range(2 * heads):
            q_s[i] = q[:, i * nkeys:(i + 1) * nkeys]

        lane_tiles = [slice(lt * LANES, (lt + 1) * LANES) for lt in range(h_ref.shape[0] // LANES)]

        def head_body(hd, carry):
            s0_all = _dg(keys_ref[2 * hd], q_s[2 * hd], NT)
            s1_all = _dg(keys_ref[2 * hd + 1], q_s[2 * hd + 1], NT)
            nt = len(lane_tiles)
            s0_l = [s0_all[:, ls] for ls in lane_tiles]
            s1_l = [s1_all[:, ls] for ls in lane_tiles]
            tops, ranks = _top_values_each(s0_l + s1_l, k, [True] * nt + [False] * nt)
            assert k == 16
            cand_l = []
            for a, b in zip(tops[:nt], tops[nt:]):
                cand_l.append(jnp.concatenate(
                    [a[0:1, :] + b, a[1:2, :] + b[0:8, :]] + [a[i:i + 1, :] + b[0:8, :] for i in range(2, 8)]
                    + [a[8:16, :] + b[0:1, :]], axis=0))
            cv_l, _ = _top_values_each(cand_l, k, [False] * nt)
            for lt in range(nt):
                s0, s1, a, b, cv, rank0 = s0_l[lt], s1_l[lt], tops[lt], tops[nt + lt], cv_l[lt], ranks[lt]
                z = jnp.sum(jnp.exp(cv - cv[0:1, :]), axis=0, keepdims=True)
                e0_s[hd, lt] = jnp.exp(s0 - a[0:1, :]) * (1.0 / z)
                e1_s[hd, lt] = jnp.exp(s1 - b[0:1, :])
                eb1 = jnp.exp(b - b[0:1, :])
                tau_r = jnp.full(a.shape, jnp.inf, F32)
                for r in range(k):
                    tau_r = jnp.where(a + b[r:r + 1, :] >= cv[k - 1:k, :], eb1[r:r + 1, :], tau_r)
                tau = jnp.full(s0.shape, jnp.inf, F32)
                for r in range(k):
                    tau = jnp.where(rank0 == float(r), tau_r[r:r + 1, :], tau)
                tau_s[hd, lt] = tau
            return carry

        lax.fori_loop(0, heads, head_body, 0)
        acc_s[...] = jnp.zeros_like(acc_s)

    per = eb // nkeys
    assert per % 8 == 0
    i0 = pl.multiple_of(e * per, 8)
    nlt = ht_s.shape[0]
    mxw = min(2, nlt)

    def pre_act(p):
        rows = slice(p * mxw * LANES, (p + 1) * mxw * LANES)
        res = _dg(u_ref[...], xn_s[rows, :], NT)
        for j in range(mxw):
            ht_s[p * mxw + j] = res[:, j * LANES:(j + 1) * LANES]

    def accumulate(p):
        cols = slice(p * mxw * LANES, (p + 1) * mxw * LANES)
        gp = jnp.concatenate([g_s[p * mxw + j] for j in range(mxw)], axis=1)
        acc_s[:, cols] += _dg(vt_ref[...], gp)

    def gate(lt):
        tau_l = [tau_s[hd, lt, pl.ds(i0, per), :] for hd in range(heads)]
        e0_l = [e0_s[hd, lt, pl.ds(i0, per), :] for hd in range(heads)]
        for il in range(per):
            w = jnp.zeros((nkeys, LANES), F32)
            for hd in range(heads):
                e1 = e1_s[hd, lt]
                w = w + jnp.where(e1 >= tau_l[hd][il:il + 1, :], e1, 0.0) * e0_l[hd][il:il + 1, :]
            g = jax.nn.gelu(ht_s[lt, il * nkeys:(il + 1) * nkeys, :]) * w
            g_s[lt, il * nkeys:(il + 1) * nkeys, :] = g.astype(BF16)

    npieces = nlt // mxw
    for p in range(npieces):
        pre_act(p)
    for lt in range(nlt):
        gate(lt)
    for p in range(npieces):
        accumulate(p)

    @pl.when(e == ne - 1)
    def _():
        out = h_ref[...] + acc_s[...].T
        y_ref[...] = out * lax.rsqrt(jnp.mean(out * out, -1, keepdims=True) + NORM_EPS) * lnf_ref[...]


def _peer(h, ln2, wq, keys, u, vt, lnf, tb, eb=2048):
    m, d = h.shape
    nq = wq.shape[1]
    nk2, nkeys, half = keys.shape
    heads = nk2 // 2
    nexp = u.shape[0]
    ne = nexp // eb
    assert tb % LANES == 0
    nlt = tb // LANES
    kern = functools.partial(_peer_kernel, heads=heads, nkeys=nkeys, eb=eb, ne=ne)
    return pl.pallas_call(
        kern,
        grid=(m // tb, ne),
        in_specs=[
            pl.BlockSpec((tb, d), lambda t, e: (t, 0)),
            pl.BlockSpec((1, d), lambda t, e: (0, 0)),
            pl.BlockSpec((d, nq), lambda t, e: (0, 0)),
            pl.BlockSpec((nk2, nkeys, half), lambda t, e: (0, 0, 0)),
            pl.BlockSpec((eb, d), lambda t, e: (e, 0)),
            pl.BlockSpec((d, eb), lambda t, e: (0, e)),
            pl.BlockSpec((1, d), lambda t, e: (0, 0)),
        ],
        out_specs=pl.BlockSpec((tb, d), lambda t, e: (t, 0)),
        out_shape=jax.ShapeDtypeStruct((m, d), F32),
        scratch_shapes=[
            pltpu.VMEM((tb, d), BF16),
            pltpu.VMEM((nk2, tb, half), BF16),
            pltpu.VMEM((heads, nlt, nkeys, LANES), F32),
            pltpu.VMEM((heads, nlt, nkeys, LANES), F32),
            pltpu.VMEM((heads, nlt, nkeys, LANES), F32),
            pltpu.VMEM((nlt, eb, LANES), F32),
            pltpu.VMEM((nlt, eb, LANES), BF16),
            pltpu.VMEM((d, tb), F32),
        ],
        compiler_params=pltpu.CompilerParams(
            dimension_semantics=("arbitrary", "arbitrary"), vmem_limit_bytes=VMEM_LIMIT),
        name="peer",
    )(h, ln2, wq, keys, u, vt, lnf)


def _pad_lanes(v, width, offset=0):
    out = jnp.zeros((width,), F32)
    return out.at[offset:offset + v.shape[0]].set(v.astype(F32))


def kernel(x_prompt, x_sample, cache_dn_conv, state_dn, cache_rw_shift, state_rw, meta_tokens, ln1, w_in, dn_conv_w, dn_a_log, dn_dt_bias, dn_norm_w, rw_mu, rw_w0, rw_w2, rw_a0, rw_a2, rw_g2, rw_k_k, rw_k_a, rw_r_k, rw_gn_w, rw_gn_b, w_up_dn, w_up_rw, w_out, ln2, peer_wq, peer_keys, peer_u, peer_v, ln_f):
    bp, seq, d = x_prompt.shape
    bs, seq_s, _ = x_sample.shape
    depth = w_in.shape[0]
    assert depth == 1 and seq_s == 1
    n_meta = meta_tokens.shape[0]
    _, _, heads_dn, dk, dv = state_dn.shape
    _, _, heads_rw, n_rw, _ = state_rw.shape
    n_conv = dn_conv_w.shape[2]
    nqkv = 2 * heads_dn * dk + heads_dn * dv
    assert n_conv == nqkv == 1536 and dn_conv_w.shape[1] == 4 and heads_dn * dv == 512
    nw = heads_rw * n_rw
    rank_w, rank_a, rank_g = rw_w2.shape[1], rw_a2.shape[1], rw_g2.shape[1]
    assert nw == 512 and rank_w + rank_a == LANES and rank_g == LANES
    n_rwp = 3 * nw + rank_w + rank_a + rank_g
    o_z = nqkv
    o_beta = o_z + heads_dn * dv
    o_alpha = o_beta + heads_dn
    o_rw = o_alpha + heads_dn
    o_gate = o_rw + n_rwp
    assert w_in.shape[2] == o_gate + 2 * d and 2 * heads_dn == 8

    wi = w_in[0]
    seg1 = jnp.concatenate([wi[:, o_rw:o_gate], wi[:, o_beta:o_rw],
                            jnp.zeros((d, 2048 - n_rwp - 2 * heads_dn), F32)], axis=1)
    w_packed = jnp.concatenate([wi[:, 0:o_beta], seg1, wi[:, o_gate:]], axis=1).astype(BF16)
    w_bt = wi[:, o_beta:o_rw].T.astype(BF16)
    g1 = ln1[0][None, :]
    cw = dn_conv_w[0]
    prow = jnp.stack([_pad_lanes(dn_a_log[0], LANES, heads_dn), _pad_lanes(dn_dt_bias[0], LANES, heads_dn)])
    pcol = jnp.stack([_pad_lanes(dn_a_log[0], 8, heads_dn), _pad_lanes(dn_dt_bias[0], 8, heads_dn)], axis=1)
    nwd = dn_norm_w[0][None, :]
    mu = rw_mu[0]
    par = jnp.stack([mu[0:nw], mu[nw:2 * nw], mu[2 * nw:3 * nw], _pad_lanes(mu[3 * nw:], nw),
                     rw_w0[0], rw_a0[0], rw_k_k[0], rw_k_a[0]])
    wwa = jnp.zeros((LANES, 2 * nw), F32)
    wwa = wwa.at[0:rank_w, 0:nw].set(rw_w2[0]).at[rank_w:, nw:].set(rw_a2[0]).astype(BF16)
    g2 = rw_g2[0].astype(BF16)
    grp = jnp.arange(nw) // n_rw
    bd = (grp[:, None] == grp[None, :]).astype(BF16)
    epi = jnp.zeros((8, nw), F32).at[0].set(rw_gn_w[0]).at[1].set(rw_gn_b[0]).at[2].set(rw_r_k[0].reshape(-1))
    wd = w_up_dn[0].astype(BF16)
    wr = w_up_rw[0].astype(BF16)
    wo = w_out[0].astype(BF16)
    l2 = ln2[0][None, :]
    wq = peer_wq[0].astype(BF16)
    pk = peer_keys[0]
    keys = pk.reshape(pk.shape[0] * pk.shape[1], pk.shape[2], pk.shape[3]).astype(BF16)
    pu = peer_u[0].astype(BF16)
    pvt = peer_v[0].T.astype(BF16)
    lf = ln_f[None, :]

    t_real = n_meta + seq
    nblk = -(-t_real // ROWS)
    t_pad = nblk * ROWS
    n_front = t_pad - t_real
    assert (n_front + n_meta) % ROWS == 0 and seq % ROWS == 0
    skip = (n_front + n_meta) // ROWS
    xp = jnp.concatenate([jnp.zeros((bp, n_front, d), F32),
                          jnp.broadcast_to(meta_tokens.astype(F32), (bp, n_meta, d)), x_prompt], axis=1)
    xp = xp.reshape(bp * t_pad, d)
    tm = max(t for t in (1024, 512, 256, ROWS) if (bp * t_pad) % t == 0)
    pp, batp = _norm_proj(xp, g1, w_packed, w_bt, tm)
    odn_p, p_dn = _dn_prompt(pp, batp, cw, prow, pcol, nwd, bp, nblk, heads_dn, dk, dv)
    orw_p, p_rw = _rw_prompt(pp, par, wwa, g2, bd, epi, bp, nblk, heads_rw, n_rw, rank_w)
    hp = _merge(xp, odn_p, orw_p, pp, wd, wr, wo, bp, nblk, seq // ROWS, skip, ROWS)
    y_prompt = _peer(hp, l2, wq, keys, pu, pvt, lf, 512).reshape(bp, seq, d)
    pp3 = pp.reshape(bp, t_pad, -1)
    p_conv = pp3[:, t_pad - 3:, 0:nqkv]
    p_shift = pp3[:, t_pad - 1:, 2048:2048 + n_rwp]

    xs = x_sample.reshape(bs, d)
    ps, _ = _norm_proj(xs, g1, w_packed, w_bt, bs)
    cache_t = jnp.moveaxis(cache_dn_conv[0], 1, 0)
    odn_s, s_dn = _dn_sample(ps, cache_t, cw, prow, nwd, state_dn[0], heads_dn, dk, dv)
    orw_s, s_rw = _rw_sample(ps, cache_rw_shift[0][:, 0, :], par, wwa, g2, bd, epi, state_rw[0],
                             heads_rw, n_rw, rank_w)
    hs = _merge(xs, odn_s, orw_s, ps, wd, wr, wo, 1, 1, 1, 0, bs)
    y_sample = _peer(hs, l2, wq, keys, pu, pvt, lf, bs).reshape(bs, 1, d)
    s_conv = jnp.concatenate([cache_dn_conv[0][:, 1:], ps[:, None, 0:nqkv]], axis=1)
    s_shift = ps[:, None, 2048:2048 + n_rwp]

    return (y_prompt, y_sample, p_conv[None], p_dn[None], p_shift[None], p_rw[None],
            s_conv[None], s_dn[None], s_shift[None], s_rw[None])
```

```python
import functools

import jax
import jax.numpy as jnp
from jax import lax
from jax.experimental import pallas as pl
from jax.experimental.pallas import tpu as pltpu

F32 = jnp.float32
BF16 = jnp.bfloat16

NORM_EPS = 1e-6
L2_EPS = 1e-6
GN_EPS = 64e-5
PEER_TOPK = 16

LANES = 128
CHUNK = 64
ROWS = 2 * CHUNK
INV_BLOCK = 16
VMEM_LIMIT = 52 * 1024 * 1024

NN = ((1,), (0,))
NT = ((1,), (1,))
TN = ((0,), (0,))


def _dg(a, b, dims=NN):
    return lax.dot_general(a, b, (dims, ((), ())), preferred_element_type=F32)


def _mm1(a, b, dims=NN):
    return _dg(a.astype(BF16), b.astype(BF16), dims)


def _hi_lo(x):
    hi = x.astype(BF16)
    lo = (x - hi.astype(F32)).astype(BF16)
    return hi, lo


def _mm3(a, b, dims=NN):
    a1, a2 = _hi_lo(a)
    b1, b2 = _hi_lo(b)
    return _dg(a1, b1, dims) + (_dg(a1, b2, dims) + _dg(a2, b1, dims))


def _split3(x):
    x1 = x.astype(BF16)
    r1 = x - x1.astype(F32)
    x2 = r1.astype(BF16)
    x3 = (r1 - x2.astype(F32)).astype(BF16)
    return x1, x2, x3


def _mm_mask_lhs(m, x, dims=NN):
    x1, x2, x3 = _split3(x)
    m = m.astype(BF16)
    return _dg(m, x1, dims) + (_dg(m, x2, dims) + _dg(m, x3, dims))


def _mm_mask_rhs(x, m, dims=NN):
    x1, x2, x3 = _split3(x)
    m = m.astype(BF16)
    return _dg(x1, m, dims) + (_dg(x2, m, dims) + _dg(x3, m, dims))


def _sigmoid(x):
    return jax.nn.sigmoid(x)


def _silu(x):
    return x * jax.nn.sigmoid(x)


def _softplus(x):
    return jnp.maximum(x, 0.0) + jnp.log(1.0 + jnp.exp(-jnp.abs(x)))


def _iota2(n, m):
    return (lax.broadcasted_iota(jnp.int32, (n, m), 0), lax.broadcasted_iota(jnp.int32, (n, m), 1))


def _mm3_each(pairs, dims=NN):
    split = [(_hi_lo(a), _hi_lo(b)) for a, b in pairs]
    return [_dg(a1, b1, dims) + (_dg(a1, b2, dims) + _dg(a2, b1, dims)) for (a1, a2), (b1, b2) in split]


def _mm1_each(pairs, dims=NN):
    cast = [(a.astype(BF16), b.astype(BF16)) for a, b in pairs]
    return [_dg(a, b, dims) for a, b in cast]


def _tri_inv_each(mats, ri, ci):
    assert mats[0].shape == (CHUNK, CHUNK) and CHUNK == 4 * INV_BLOCK and INV_BLOCK == 16
    eye = jnp.where(ri == ci, 1.0, 0.0).astype(F32)
    same = lax.shift_right_logical(ri, 4) == lax.shift_right_logical(ci, 4)
    n = [jnp.where(same, a, 0.0) for a in mats]
    e = [a - x for a, x in zip(mats, n)]
    n2 = _mm1_each(list(zip(n, n)))
    n4 = _mm1_each(list(zip(n2, n2)))
    n8 = _mm1_each(list(zip(n4, n4)))
    d = _mm1_each([(eye - x, eye + y) for x, y in zip(n, n2)])
    d = _mm1_each([(x, eye + y) for x, y in zip(d, n4)])
    d = _mm1_each([(x, eye + y) for x, y in zip(d, n8)])
    f = _mm1_each(list(zip(d, e)))
    f2 = _mm1_each(list(zip(f, f)))
    x = _mm1_each([(eye - p, eye + q) for p, q in zip(f, f2)])
    return _mm1_each(list(zip(x, d)))


def _chunk_masks():
    ri, ci = _iota2(ROWS, ROWS)
    same = lax.shift_right_logical(ri, 6) == lax.shift_right_logical(ci, 6)
    low = jnp.where(same, jnp.where(ci <= ri, 1.0, 0.0), 0.0).astype(BF16)
    up = jnp.where(same, jnp.where(ri <= ci, 1.0, 0.0), 0.0).astype(BF16)
    return low, up


def _norm_proj_kernel(x_ref, g_ref, w_ref, wbt_ref, o_ref, bt_ref, xn_ref):
    @pl.when(pl.program_id(1) == 0)
    def _():
        x = x_ref[...]
        xn = x * lax.rsqrt(jnp.mean(x * x, -1, keepdims=True) + NORM_EPS) * g_ref[...]
        xn_ref[...] = xn.astype(BF16)
        bt_ref[...] = _dg(wbt_ref[...], xn_ref[...], NT)
    o_ref[...] = _dg(xn_ref[...], w_ref[...])


def _norm_proj(x, g, w_packed, w_bt, tm):
    m, d = x.shape
    n = w_packed.shape[1]
    tn = 2048
    return pl.pallas_call(
        _norm_proj_kernel,
        grid=(m // tm, n // tn),
        in_specs=[
            pl.BlockSpec((tm, d), lambda i, j: (i, 0)),
            pl.BlockSpec((1, d), lambda i, j: (0, 0)),
            pl.BlockSpec((d, tn), lambda i, j: (0, j)),
            pl.BlockSpec((8, d), lambda i, j: (0, 0)),
        ],
        out_specs=[
            pl.BlockSpec((tm, tn), lambda i, j: (i, j)),
            pl.BlockSpec((8, tm), lambda i, j: (0, i)),
        ],
        out_shape=[jax.ShapeDtypeStruct((m, n), F32), jax.ShapeDtypeStruct((8, m), F32)],
        scratch_shapes=[pltpu.VMEM((tm, d), BF16)],
        compiler_params=pltpu.CompilerParams(
            dimension_semantics=("arbitrary", "arbitrary"), vmem_limit_bytes=VMEM_LIMIT),
        name="norm_proj",
    )(x, g, w_packed, w_bt)


def _l2norm_rows(x):
    return x * lax.rsqrt(jnp.sum(x * x, -1, keepdims=True) + L2_EPS)


def _dn_out_norm(o, z, nw):
    on = o * lax.rsqrt(jnp.mean(o * o, -1, keepdims=True) + NORM_EPS) * nw
    return on * _silu(z)


def _dn_prompt_kernel(p0_ref, ba_ref, bat_ref, pm0_ref, bam_ref, batm_ref, cw_ref, prow_ref, pcol_ref, nw_ref,
                      o_ref, s_out_ref, s_ref, cbuf_ref, *, heads, dk, dv):
    c = pl.program_id(1)
    nqk = heads * dk
    nconv = 2 * nqk + heads * dv

    @pl.when(c == 0)
    def _():
        s_ref[...] = jnp.zeros_like(s_ref)
        cbuf_ref[0:8, :] = jnp.zeros((8, nconv), F32)

    first = c == 0
    p0 = jnp.where(first, pm0_ref[...], p0_ref[...])
    u = p0[:, 0:nconv]
    cbuf_ref[8:8 + ROWS, :] = u
    y = (cw_ref[3:4, :] * u + cw_ref[2:3, :] * cbuf_ref[7:7 + ROWS, :]
         + cw_ref[1:2, :] * cbuf_ref[6:6 + ROWS, :] + cw_ref[0:1, :] * cbuf_ref[5:5 + ROWS, :])
    cbuf_ref[0:8, :] = u[ROWS - 8:ROWS, :]
    qkv = _silu(y)

    ba = jnp.where(first, bam_ref[...], ba_ref[...])
    beta_all = _sigmoid(ba)
    g_col = -jnp.exp(prow_ref[0:1, :]) * _softplus(ba + prow_ref[1:2, :])
    bat = jnp.where(first, batm_ref[...], bat_ref[...])
    g_row = -jnp.exp(pcol_ref[:, 0:1]) * _softplus(bat + pcol_ref[:, 1:2])
    low, up = _chunk_masks()
    gc_all = _mm_mask_lhs(low, g_col)
    gr_all = _mm_mask_rhs(g_row, up)
    ri, ci = _iota2(CHUNK, CHUNK)
    scale = dk ** -0.5

    hcs = [(s, h) for s in range(ROWS // CHUNK) for h in range(heads)]
    q_l, k_l, kb_l, vb_l, gc_l, dec_l = [], [], [], [], [], []
    for s, h in hcs:
        r0 = s * CHUNK
        q_l.append(_l2norm_rows(qkv[r0:r0 + CHUNK, h * dk:(h + 1) * dk]) * scale)
        k = _l2norm_rows(qkv[r0:r0 + CHUNK, nqk + h * dk:nqk + (h + 1) * dk])
        beta = beta_all[r0:r0 + CHUNK, h:h + 1]
        gc = gc_all[r0:r0 + CHUNK, heads + h:heads + h + 1]
        gr = gr_all[heads + h:heads + h + 1, r0:r0 + CHUNK]
        k_l.append(k)
        kb_l.append(k * beta)
        vb_l.append(qkv[r0:r0 + CHUNK, 2 * nqk + h * dv:2 * nqk + (h + 1) * dv] * beta)
        gc_l.append(gc)
        dec_l.append(jnp.where(ci <= ri, jnp.exp(jnp.minimum(gc - gr, 0.0)), 0.0))
    a_l = [jnp.where(ci < ri, x * dec, 0.0)
           for x, dec in zip(_mm1_each(list(zip(kb_l, k_l)), NT), dec_l)]
    t_l = _tri_inv_each(a_l, ri, ci)
    eg_l = [jnp.exp(gc) for gc in gc_l]
    uu_l = _mm1_each(list(zip(t_l, vb_l)))
    ww_l = _mm1_each([(t, kb * eg) for t, kb, eg in zip(t_l, kb_l, eg_l)])
    qk_l = [jnp.where(ci <= ri, x * dec, 0.0)
            for x, dec in zip(_mm1_each(list(zip(q_l, k_l)), NT), dec_l)]

    for s in range(ROWS // CHUNK):
        r0 = s * CHUNK
        ix = [s * heads + h for h in range(heads)]
        st_l = [s_ref[h] for h in range(heads)]
        ws_l = _mm1_each([(ww_l[i], st) for i, st in zip(ix, st_l)])
        qs_l = _mm1_each([(q_l[i] * eg_l[i], st) for i, st in zip(ix, st_l)])
        vn_l = [uu_l[i] - ws for i, ws in zip(ix, ws_l)]
        qv_l = _mm1_each([(qk_l[i], vn) for i, vn in zip(ix, vn_l)])
        gl_l = [gc_l[i][CHUNK - 1:CHUNK, :] for i in ix]
        kv_l = _mm1_each([(k_l[i] * jnp.exp(gl - gc_l[i]), vn) for i, gl, vn in zip(ix, gl_l, vn_l)], TN)
        for h in range(heads):
            s_ref[h] = st_l[h] * jnp.exp(gl_l[h]) + kv_l[h]
            z = p0[r0:r0 + CHUNK, nconv + h * dv:nconv + (h + 1) * dv]
            o_ref[r0:r0 + CHUNK, h * dv:(h + 1) * dv] = _dn_out_norm(
                qs_l[h] + qv_l[h], z, nw_ref[...]).astype(o_ref.dtype)

    @pl.when(c == pl.num_programs(1) - 1)
    def _():
        s_out_ref[0] = s_ref[...]


def _prompt_block(nreal):
    return lambda b, c: b * nreal + jnp.maximum(c - 1, 0)


def _dn_prompt(p, bat, pm, batm, cw, prow, pcol, nw, batch, nreal, heads, dk, dv):
    m = p.shape[0]
    nconv = 2 * heads * dk + heads * dv
    ba_blk = (2048 + 1792) // LANES
    kern = functools.partial(_dn_prompt_kernel, heads=heads, dk=dk, dv=dv)
    blk = _prompt_block(nreal)
    return pl.pallas_call(
        kern,
        grid=(batch, nreal + 1),
        in_specs=[
            pl.BlockSpec((ROWS, 2048), lambda b, c: (blk(b, c), 0)),
            pl.BlockSpec((ROWS, LANES), lambda b, c: (blk(b, c), ba_blk)),
            pl.BlockSpec((8, ROWS), lambda b, c: (0, blk(b, c))),
            pl.BlockSpec((ROWS, 2048), lambda b, c: (0, 0)),
            pl.BlockSpec((ROWS, LANES), lambda b, c: (0, ba_blk)),
            pl.BlockSpec((8, ROWS), lambda b, c: (0, 0)),
            pl.BlockSpec((4, nconv), lambda b, c: (0, 0)),
            pl.BlockSpec((2, LANES), lambda b, c: (0, 0)),
            pl.BlockSpec((8, 2), lambda b, c: (0, 0)),
            pl.BlockSpec((1, dv), lambda b, c: (0, 0)),
        ],
        out_specs=[
            pl.BlockSpec((ROWS, heads * dv), lambda b, c: (blk(b, c), 0)),
            pl.BlockSpec((1, heads, dk, dv), lambda b, c: (b, 0, 0, 0)),
        ],
        out_shape=[jax.ShapeDtypeStruct((m, heads * dv), BF16),
                   jax.ShapeDtypeStruct((batch, heads, dk, dv), F32)],
        scratch_shapes=[pltpu.VMEM((heads, dk, dv), F32), pltpu.VMEM((8 + ROWS, nconv), F32)],
        compiler_params=pltpu.CompilerParams(
            dimension_semantics=("arbitrary", "arbitrary"), vmem_limit_bytes=VMEM_LIMIT),
        name="dn_prompt",
    )(p, p, bat, pm, pm, batm, cw, prow, pcol, nw)


def _col_from_row(row, eye):
    n = row.shape[1]
    return jnp.sum(jnp.where(eye, jnp.broadcast_to(row, (n, n)), 0.0), axis=1, keepdims=True)


def _row_from_col(col, eye):
    n = col.shape[0]
    return jnp.sum(jnp.where(eye, jnp.broadcast_to(col, (n, n)), 0.0), axis=0, keepdims=True)


def _dn_sample_kernel(p0_ref, ba_ref, cache_ref, cw_ref, prow_ref, nw_ref, s_in_ref,
                      o_ref, s_out_ref, q_s, k_s, v_s, g_s, b_s, o_s, *, heads, dk, dv, tb):
    nqk = heads * dk
    nconv = 2 * nqk + heads * dv
    u = p0_ref[:, 0:nconv]
    y = (cw_ref[3:4, :] * u + cw_ref[2:3, :] * cache_ref[2] + cw_ref[1:2, :] * cache_ref[1]
         + cw_ref[0:1, :] * cache_ref[0])
    qkv = _silu(y)
    ba = ba_ref[...]
    b_s[...] = _sigmoid(ba)
    g_s[...] = -jnp.exp(prow_ref[0:1, :]) * _softplus(ba + prow_ref[1:2, :])
    scale = dk ** -0.5
    for h in range(heads):
        q_s[:, h * dk:(h + 1) * dk] = _l2norm_rows(qkv[:, h * dk:(h + 1) * dk]) * scale
        k_s[:, h * dk:(h + 1) * dk] = _l2norm_rows(qkv[:, nqk + h * dk:nqk + (h + 1) * dk])
    v_s[...] = qkv[:, 2 * nqk:]
    ri, ci = _iota2(dk, dk)
    eye = ri == ci

    for i in range(tb):
        for h in range(heads):
            k_row = k_s[i:i + 1, h * dk:(h + 1) * dk]
            q_row = q_s[i:i + 1, h * dk:(h + 1) * dk]
            v_row = v_s[i:i + 1, h * dv:(h + 1) * dv]
            g = g_s[i:i + 1, heads + h:heads + h + 1]
            beta = b_s[i:i + 1, h:h + 1]
            k_col = _col_from_row(k_row, eye)
            q_col = _col_from_row(q_row, eye)
            st = s_in_ref[i, h] * jnp.exp(g)
            kv = jnp.sum(st * k_col, axis=0, keepdims=True)
            delta = (v_row - kv) * beta
            st = st + k_col * delta
            s_out_ref[i, h] = st
            o_s[i:i + 1, h * dv:(h + 1) * dv] = jnp.sum(st * q_col, axis=0, keepdims=True)
    for h in range(heads):
        z = p0_ref[:, nconv + h * dv:nconv + (h + 1) * dv]
        o_ref[:, h * dv:(h + 1) * dv] = _dn_out_norm(
            o_s[:, h * dv:(h + 1) * dv], z, nw_ref[...]).astype(o_ref.dtype)


def _dn_sample(p, cache_t, cw, prow, nw, state, heads, dk, dv, tb=8):
    m = p.shape[0]
    nconv = 2 * heads * dk + heads * dv
    ba_blk = (2048 + 1792) // LANES
    kern = functools.partial(_dn_sample_kernel, heads=heads, dk=dk, dv=dv, tb=tb)
    return pl.pallas_call(
        kern,
        grid=(m // tb,),
        in_specs=[
            pl.BlockSpec((tb, 2048), lambda i: (i, 0)),
            pl.BlockSpec((tb, LANES), lambda i: (i, ba_blk)),
            pl.BlockSpec((3, tb, nconv), lambda i: (0, i, 0)),
            pl.BlockSpec((4, nconv), lambda i: (0, 0)),
            pl.BlockSpec((2, LANES), lambda i: (0, 0)),
            pl.BlockSpec((1, dv), lambda i: (0, 0)),
            pl.BlockSpec((tb, heads, dk, dv), lambda i: (i, 0, 0, 0)),
        ],
        out_specs=[
            pl.BlockSpec((tb, heads * dv), lambda i: (i, 0)),
            pl.BlockSpec((tb, heads, dk, dv), lambda i: (i, 0, 0, 0)),
        ],
        out_shape=[jax.ShapeDtypeStruct((m, heads * dv), BF16),
                   jax.ShapeDtypeStruct(state.shape, F32)],
        scratch_shapes=[pltpu.VMEM((tb, heads * dk), F32), pltpu.VMEM((tb, heads * dk), F32),
                        pltpu.VMEM((tb, heads * dv), F32), pltpu.VMEM((tb, LANES), F32),
                        pltpu.VMEM((tb, LANES), F32), pltpu.VMEM((tb, heads * dv), F32)],
        compiler_params=pltpu.CompilerParams(
            dimension_semantics=("arbitrary",), vmem_limit_bytes=VMEM_LIMIT),
        name="dn_sample",
    )(p, p, cache_t, cw, prow, nw, state)


def _group_sum(x, bd):
    return _mm_mask_rhs(x, bd)


def _rw_prep(prw, prev, par_ref, wwa_ref, g2_ref, bd, nw, rank_w):
    r = prw[:, 0:nw] + (prev[:, 0:nw] - prw[:, 0:nw]) * par_ref[0:1, :]
    kr = prw[:, nw:2 * nw] + (prev[:, nw:2 * nw] - prw[:, nw:2 * nw]) * par_ref[1:2, :]
    vr = prw[:, 2 * nw:3 * nw] + (prev[:, 2 * nw:3 * nw] - prw[:, 2 * nw:3 * nw]) * par_ref[2:3, :]
    c0 = 3 * nw
    lo = prw[:, c0:c0 + 2 * LANES]
    lo = lo + (prev[:, c0:c0 + 2 * LANES] - lo) * par_ref[3:4, 0:2 * LANES]
    wa = lo[:, 0:LANES]
    lane = lax.broadcasted_iota(jnp.int32, wa.shape, 1)
    wa = jnp.where(lane < rank_w, jnp.tanh(wa), wa)
    proj = _mm1(wa, wwa_ref[...])
    w_log = -_softplus(-(par_ref[4:5, :] + proj[:, 0:nw])) - 0.5
    logdec = -jnp.exp(w_log)
    a = _sigmoid(par_ref[5:6, :] + proj[:, nw:2 * nw])
    gate = _mm1(_sigmoid(lo[:, LANES:2 * LANES]), g2_ref[...])
    kk = kr * par_ref[6:7, :]
    kk = kk * lax.rsqrt(_group_sum(kk * kk, bd) + L2_EPS)
    k2 = kr * (1.0 + (a - 1.0) * par_ref[7:8, :])
    return r, logdec, k2, vr, -kk, kk * a, gate


def _rw_epilogue(y, r, k2, v, gate, epi_ref, bd, n):
    mu = _group_sum(y, bd) * (1.0 / n)
    d = y - mu
    var = _group_sum(d * d, bd) * (1.0 / n)
    yn = d * lax.rsqrt(var + GN_EPS) * epi_ref[0:1, :] + epi_ref[1:2, :]
    bonus = _group_sum(r * k2 * epi_ref[2:3, :], bd) * v
    return (yn + bonus) * gate


def _rw_prompt_kernel(p1_ref, pm1_ref, par_ref, wwa_ref, g2_ref, bd_ref, epi_ref,
                      o_ref, s_out_ref, s_ref, sbuf_ref, y_s, *, heads, n, rank_w):
    c = pl.program_id(1)
    nw = heads * n
    nrw = 3 * nw + 2 * LANES

    @pl.when(c == 0)
    def _():
        s_ref[...] = jnp.zeros_like(s_ref)
        sbuf_ref[0:8, :] = jnp.zeros((8, nrw), F32)

    prw = jnp.where(c == 0, pm1_ref[:, 0:nrw], p1_ref[:, 0:nrw])
    sbuf_ref[8:8 + ROWS, :] = prw
    prev = sbuf_ref[7:7 + ROWS, :]
    sbuf_ref[0:8, :] = prw[ROWS - 8:ROWS, :]
    bd = bd_ref[...]
    r, logdec, k2, v, a_in, b_in, gate = _rw_prep(prw, prev, par_ref, wwa_ref, g2_ref, bd, nw, rank_w)
    low, _ = _chunk_masks()
    g_all = _mm_mask_lhs(low, logdec)
    ri, ci = _iota2(CHUNK, CHUNK)

    hcs = [(s, h) for s in range(ROWS // CHUNK) for h in range(heads)]
    at_l, rt_l, bt_l, kt_l, bd_l, kd_l, vv_l, egl_l = [], [], [], [], [], [], [], []
    for s, h in hcs:
        sl = (slice(s * CHUNK, (s + 1) * CHUNK), slice(h * n, (h + 1) * n))
        g = g_all[sl]
        eg = jnp.exp(g)
        ieg = jnp.exp(-g)
        gl = g[CHUNK - 1:CHUNK, :]
        tail = jnp.exp(gl - g)
        at_l.append(a_in[sl] * jnp.exp(g - logdec[sl]))
        rt_l.append(r[sl] * eg)
        bt_l.append(b_in[sl] * ieg)
        kt_l.append(k2[sl] * ieg)
        bd_l.append(b_in[sl] * tail)
        kd_l.append(k2[sl] * tail)
        vv_l.append(v[sl])
        egl_l.append(jnp.exp(gl))
    nab_l = [jnp.where(ci < ri, -x, 0.0) for x in _mm1_each(list(zip(at_l, bt_l)), NT)]
    aak_l = [jnp.where(ci < ri, x, 0.0) for x in _mm1_each(list(zip(at_l, kt_l)), NT)]
    rb_l = [jnp.where(ci <= ri, x, 0.0) for x in _mm1_each(list(zip(rt_l, bt_l)), NT)]
    rk_l = [jnp.where(ci <= ri, x, 0.0) for x in _mm1_each(list(zip(rt_l, kt_l)), NT)]
    t_l = _tri_inv_each(nab_l, ri, ci)
    akv_l = _mm1_each(list(zip(aak_l, vv_l)))
    rkv_l = _mm1_each(list(zip(rk_l, vv_l)))
    vkd_l = _mm1_each(list(zip(vv_l, kd_l)), TN)

    for s in range(ROWS // CHUNK):
        r0 = s * CHUNK
        ix = [s * heads + h for h in range(heads)]
        st_l = [s_ref[h] for h in range(heads)]
        as_l = _mm1_each([(at_l[i], st) for i, st in zip(ix, st_l)], NT)
        rs_l = _mm1_each([(rt_l[i], st) for i, st in zip(ix, st_l)], NT)
        uu_l = _mm1_each([(t_l[i], x + akv_l[i]) for i, x in zip(ix, as_l)])
        ru_l = _mm1_each([(rb_l[i], uu) for i, uu in zip(ix, uu_l)])
        ub_l = _mm1_each([(uu, bd_l[i]) for i, uu in zip(ix, uu_l)], TN)
        for h in range(heads):
            i = ix[h]
            y_s[r0:r0 + CHUNK, h * n:(h + 1) * n] = rs_l[h] + ru_l[h] + rkv_l[i]
            s_ref[h] = st_l[h] * egl_l[i] + ub_l[h] + vkd_l[i]

    o_ref[...] = _rw_epilogue(y_s[...], r, k2, v, gate, epi_ref, bd, n).astype(o_ref.dtype)

    @pl.when(c == pl.num_programs(1) - 1)
    def _():
        s_out_ref[0] = s_ref[...]


def _rw_prompt(p, pm, par, wwa, g2, bd, epi, batch, nreal, heads, n, rank_w):
    m = p.shape[0]
    nw = heads * n
    nrw = 3 * nw + 2 * LANES
    kern = functools.partial(_rw_prompt_kernel, heads=heads, n=n, rank_w=rank_w)
    blk = _prompt_block(nreal)
    return pl.pallas_call(
        kern,
        grid=(batch, nreal + 1),
        in_specs=[
            pl.BlockSpec((ROWS, 2048), lambda b, c: (blk(b, c), 1)),
            pl.BlockSpec((ROWS, 2048), lambda b, c: (0, 1)),
            pl.BlockSpec((8, nw), lambda b, c: (0, 0)),
            pl.BlockSpec((LANES, 2 * nw), lambda b, c: (0, 0)),
            pl.BlockSpec((LANES, nw), lambda b, c: (0, 0)),
            pl.BlockSpec((nw, nw), lambda b, c: (0, 0)),
            pl.BlockSpec((8, nw), lambda b, c: (0, 0)),
        ],
        out_specs=[
            pl.BlockSpec((ROWS, nw), lambda b, c: (blk(b, c), 0)),
            pl.BlockSpec((1, heads, n, n), lambda b, c: (b, 0, 0, 0)),
        ],
        out_shape=[jax.ShapeDtypeStruct((m, nw), BF16),
                   jax.ShapeDtypeStruct((batch, heads, n, n), F32)],
        scratch_shapes=[pltpu.VMEM((heads, n, n), F32), pltpu.VMEM((8 + ROWS, nrw), F32),
                        pltpu.VMEM((ROWS, nw), F32)],
        compiler_params=pltpu.CompilerParams(
            dimension_semantics=("arbitrary", "arbitrary"), vmem_limit_bytes=VMEM_LIMIT),
        name="rw_prompt",
    )(p, pm, par, wwa, g2, bd, epi)


def _rw_sample_kernel(p1_ref, prev_ref, par_ref, wwa_ref, g2_ref, bd_ref, epi_ref, s_in_ref,
                      o_ref, s_out_ref, r_s, w_s, k_s, v_s, a_s, b_s, y_s, *, heads, n, rank_w, tb):
    nw = heads * n
    nrw = 3 * nw + 2 * LANES
    bd = bd_ref[...]
    prw = p1_ref[:, 0:nrw]
    r, logdec, k2, v, a_in, b_in, gate = _rw_prep(prw, prev_ref[...], par_ref, wwa_ref, g2_ref, bd, nw, rank_w)
    r_s[...] = r
    w_s[...] = jnp.exp(logdec)
    k_s[...] = k2
    v_s[...] = v
    a_s[...] = a_in
    b_s[...] = b_in
    ri, ci = _iota2(n, n)
    eye = ri == ci

    group = 8
    for i0 in range(0, tb, group):
        items = [(i, h) for i in range(i0, i0 + group) for h in range(heads)]
        sls = [(slice(i, i + 1), slice(h * n, (h + 1) * n)) for i, h in items]
        st_l = [s_in_ref[i, h] for i, h in items]
        sa_l = [jnp.sum(st * a_s[sl], axis=1, keepdims=True) for st, sl in zip(st_l, sls)]
        vc_l = [_col_from_row(v_s[sl], eye) for sl in sls]
        st_l = [st * w_s[sl] + sa * b_s[sl] + vc * k_s[sl] for st, sa, vc, sl in zip(st_l, sa_l, vc_l, sls)]
        yc_l = [jnp.sum(st * r_s[sl], axis=1, keepdims=True) for st, sl in zip(st_l, sls)]
        for (i, h), sl, st, yc in zip(items, sls, st_l, yc_l):
            s_out_ref[i, h] = st
            y_s[sl] = _row_from_col(yc, eye)
    o_ref[...] = _rw_epilogue(y_s[...], r, k2, v, gate, epi_ref, bd, n).astype(o_ref.dtype)


def _rw_sample(p, prev, par, wwa, g2, bd, epi, state, heads, n, rank_w, tb=8):
    m = p.shape[0]
    nw = heads * n
    nrw = 3 * nw + 2 * LANES
    kern = functools.partial(_rw_sample_kernel, heads=heads, n=n, rank_w=rank_w, tb=tb)
    return pl.pallas_call(
        kern,
        grid=(m // tb,),
        in_specs=[
            pl.BlockSpec((tb, 2048), lambda i: (i, 1)),
            pl.BlockSpec((tb, nrw), lambda i: (i, 0)),
            pl.BlockSpec((8, nw), lambda i: (0, 0)),
            pl.BlockSpec((LANES, 2 * nw), lambda i: (0, 0)),
            pl.BlockSpec((LANES, nw), lambda i: (0, 0)),
            pl.BlockSpec((nw, nw), lambda i: (0, 0)),
            pl.BlockSpec((8, nw), lambda i: (0, 0)),
            pl.BlockSpec((tb, heads, n, n), lambda i: (i, 0, 0, 0)),
        ],
        out_specs=[
            pl.BlockSpec((tb, nw), lambda i: (i, 0)),
            pl.BlockSpec((tb, heads, n, n), lambda i: (i, 0, 0, 0)),
        ],
        out_shape=[jax.ShapeDtypeStruct((m, nw), BF16), jax.ShapeDtypeStruct(state.shape, F32)],
        scratch_shapes=[pltpu.VMEM((tb, nw), F32)] * 7,
        compiler_params=pltpu.CompilerParams(
            dimension_semantics=("arbitrary",), vmem_limit_bytes=VMEM_LIMIT),
        name="rw_sample",
    )(p, prev, par, wwa, g2, bd, epi, state)


def _merge_kernel(x_ref, odn_ref, orw_ref, gt_ref, wd_ref, wr_ref, wo_ref, h_ref, *, d):
    ga = _sigmoid(gt_ref[:, 0:d])
    gb = _sigmoid(gt_ref[:, d:2 * d])
    m = ga * _dg(odn_ref[...], wd_ref[...]) + gb * _dg(orw_ref[...], wr_ref[...])
    h_ref[...] = x_ref[...] + _dg(m.astype(BF16), wo_ref[...])


def _merge(x, odn, orw, p, wd, wr, wo, batch, nblk_in, nblk_out, skip, tm):
    d = x.shape[1]
    nd = odn.shape[1]
    nr = orw.shape[1]
    kern = functools.partial(_merge_kernel, d=d)
    in_row = lambda b, i: (b * nblk_in + skip + i, 0)
    return pl.pallas_call(
        kern,
        grid=(batch, nblk_out),
        in_specs=[
            pl.BlockSpec((tm, d), in_row),
            pl.BlockSpec((tm, nd), in_row),
            pl.BlockSpec((tm, nr), in_row),
            pl.BlockSpec((tm, 2 * d), lambda b, i: (b * nblk_in + skip + i, 2)),
            pl.BlockSpec((nd, d), lambda b, i: (0, 0)),
            pl.BlockSpec((nr, d), lambda b, i: (0, 0)),
            pl.BlockSpec((d, d), lambda b, i: (0, 0)),
        ],
        out_specs=pl.BlockSpec((tm, d), lambda b, i: (b * nblk_out + i, 0)),
        out_shape=jax.ShapeDtypeStruct((batch * nblk_out * tm, d), F32),
        compiler_params=pltpu.CompilerParams(
            dimension_semantics=("arbitrary", "arbitrary"), vmem_limit_bytes=VMEM_LIMIT),
        name="merge",
    )(x, odn, orw, p, wd, wr, wo)


def _top_values_each(xs, k, ranked):
    rows = [[] for _ in xs]
    cur = list(xs)
    rank = [jnp.full(x.shape, float(k), F32) if rk else None for x, rk in zip(xs, ranked)]
    for r in range(k):
        top = [jnp.max(c, axis=0, keepdims=True) for c in cur]
        hit = [c == m for c, m in zip(cur, top)]
        for i, m in enumerate(top):
            rows[i].append(m)
            if ranked[i]:
                rank[i] = jnp.where(hit[i], float(r), rank[i])
        cur = [jnp.where(h, -jnp.inf, c) for h, c in zip(hit, cur)]
    return [jnp.concatenate(r, axis=0) for r in rows], rank


def _peer_kernel(h_ref, ln2_ref, wq_ref, keys_ref, u_ref, vt_ref, lnf_ref, y_ref,
                 xn_s, q_s, e0_s, e1_s, tau_s, ht_s, g_s, acc_s, *, heads, nkeys, eb, ne):
    e = pl.program_id(1)
    k = PEER_TOPK

    @pl.when(e == 0)
    def _():
        x = h_ref[...]
        xn = x * lax.rsqrt(jnp.mean(x * x, -1, keepdims=True) + NORM_EPS) * ln2_ref[...]
        xn_s[...] = xn.astype(BF16)
        q = _dg(xn_s[...], wq_ref[...]).astype(BF16)
        for i in range(2 * heads):
            q_s[i] = q[:, i * nkeys:(i + 1) * nkeys]

        lane_tiles = [slice(lt * LANES, (lt + 1) * LANES) for lt in range(h_ref.shape[0] // LANES)]

        def head_body(hd, carry):
            s0_all = _dg(keys_ref[2 * hd], q_s[2 * hd], NT)
            s1_all = _dg(keys_ref[2 * hd + 1], q_s[2 * hd + 1], NT)
            nt = len(lane_tiles)
            s0_l = [s0_all[:, ls] for ls in lane_tiles]
            s1_l = [s1_all[:, ls] for ls in lane_tiles]
            tops, _ = _top_values_each(s0_l + s1_l, k, [False] * (2 * nt))
            assert k == 16
            cand_l = []
            for a, b in zip(tops[:nt], tops[nt:]):
                cand_l.append(jnp.concatenate(
                    [a[0:1, :] + b, a[1:2, :] + b[0:8, :]] + [a[i:i + 1, :] + b[0:8, :] for i in range(2, 8)]
                    + [a[8:16, :] + b[0:1, :]], axis=0))
            cv_l, _ = _top_values_each(cand_l, k, [False] * nt)
            for lt in range(nt):
                s0, s1, a, b, cv = s0_l[lt], s1_l[lt], tops[lt], tops[nt + lt], cv_l[lt]
                z = jnp.sum(jnp.exp(cv - cv[0:1, :]), axis=0, keepdims=True)
                e0_s[hd, lt] = jnp.exp(s0 - a[0:1, :]) * (1.0 / z)
                e1_s[hd, lt] = jnp.exp(s1 - b[0:1, :])
                eb1 = jnp.exp(b - b[0:1, :])
                tau_r = jnp.full(a.shape, jnp.inf, F32)
                for r in range(k):
                    tau_r = jnp.where(a + b[r:r + 1, :] >= cv[k - 1:k, :], eb1[r:r + 1, :], tau_r)
                tau = jnp.full(s0.shape, jnp.inf, F32)
                for r in reversed(range(k)):
                    tau = jnp.where(s0 >= a[r:r + 1, :], tau_r[r:r + 1, :], tau)
                tau_s[hd, lt] = tau
            return carry

        lax.fori_loop(0, heads, head_body, 0)
        acc_s[...] = jnp.zeros_like(acc_s)

    per = eb // nkeys
    assert per % 8 == 0
    i0 = pl.multiple_of(e * per, 8)
    nlt = ht_s.shape[0]
    mxw = min(2, nlt)

    def pre_act(p):
        rows = slice(p * mxw * LANES, (p + 1) * mxw * LANES)
        res = _dg(u_ref[...], xn_s[rows, :], NT)
        for j in range(mxw):
            ht_s[p * mxw + j] = res[:, j * LANES:(j + 1) * LANES]

    def accumulate(p):
        cols = slice(p * mxw * LANES, (p + 1) * mxw * LANES)
        gp = jnp.concatenate([g_s[p * mxw + j] for j in range(mxw)], axis=1)
        acc_s[:, cols] += _dg(vt_ref[...], gp)

    def gate(lt):
        tau_l = [tau_s[hd, lt, pl.ds(i0, per), :] for hd in range(heads)]
        e0_l = [e0_s[hd, lt, pl.ds(i0, per), :] for hd in range(heads)]
        for il in range(per):
            w = None
            for hd in range(heads):
                e1 = e1_s[hd, lt]
                wh = jnp.where(e1 >= tau_l[hd][il:il + 1, :], e1, 0.0) * e0_l[hd][il:il + 1, :]
                w = wh if w is None else w + wh
            g = jax.nn.gelu(ht_s[lt, il * nkeys:(il + 1) * nkeys, :]) * w
            g_s[lt, il * nkeys:(il + 1) * nkeys, :] = g.astype(BF16)

    npieces = nlt // mxw
    for p in range(npieces):
        pre_act(p)
    for lt in range(nlt):
        gate(lt)
    for p in range(npieces):
        accumulate(p)

    @pl.when(e == ne - 1)
    def _():
        out = h_ref[...] + acc_s[...].T
        y_ref[...] = out * lax.rsqrt(jnp.mean(out * out, -1, keepdims=True) + NORM_EPS) * lnf_ref[...]


def _peer(h, ln2, wq, keys, u, vt, lnf, tb, eb=2048):
    m, d = h.shape
    nq = wq.shape[1]
    nk2, nkeys, half = keys.shape
    heads = nk2 // 2
    nexp = u.shape[0]
    ne = nexp // eb
    assert tb % LANES == 0
    nlt = tb // LANES
    kern = functools.partial(_peer_kernel, heads=heads, nkeys=nkeys, eb=eb, ne=ne)
    return pl.pallas_call(
        kern,
        grid=(m // tb, ne),
        in_specs=[
            pl.BlockSpec((tb, d), lambda t, e: (t, 0)),
            pl.BlockSpec((1, d), lambda t, e: (0, 0)),
            pl.BlockSpec((d, nq), lambda t, e: (0, 0)),
            pl.BlockSpec((nk2, nkeys, half), lambda t, e: (0, 0, 0)),
            pl.BlockSpec((eb, d), lambda t, e: (e, 0)),
            pl.BlockSpec((d, eb), lambda t, e: (0, e)),
            pl.BlockSpec((1, d), lambda t, e: (0, 0)),
        ],
        out_specs=pl.BlockSpec((tb, d), lambda t, e: (t, 0)),
        out_shape=jax.ShapeDtypeStruct((m, d), F32),
        scratch_shapes=[
            pltpu.VMEM((tb, d), BF16),
            pltpu.VMEM((nk2, tb, half), BF16),
            pltpu.VMEM((heads, nlt, nkeys, LANES), F32),
            pltpu.VMEM((heads, nlt, nkeys, LANES), F32),
            pltpu.VMEM((heads, nlt, nkeys, LANES), F32),
            pltpu.VMEM((nlt, eb, LANES), F32),
            pltpu.VMEM((nlt, eb, LANES), BF16),
            pltpu.VMEM((d, tb), F32),
        ],
        compiler_params=pltpu.CompilerParams(
            dimension_semantics=("arbitrary", "arbitrary"), vmem_limit_bytes=VMEM_LIMIT),
        name="peer",
    )(h, ln2, wq, keys, u, vt, lnf)


def _pad_lanes(v, width, offset=0):
    out = jnp.zeros((width,), F32)
    return out.at[offset:offset + v.shape[0]].set(v.astype(F32))


def kernel(x_prompt, x_sample, cache_dn_conv, state_dn, cache_rw_shift, state_rw, meta_tokens, ln1, w_in, dn_conv_w, dn_a_log, dn_dt_bias, dn_norm_w, rw_mu, rw_w0, rw_w2, rw_a0, rw_a2, rw_g2, rw_k_k, rw_k_a, rw_r_k, rw_gn_w, rw_gn_b, w_up_dn, w_up_rw, w_out, ln2, peer_wq, peer_keys, peer_u, peer_v, ln_f):
    bp, seq, d = x_prompt.shape
    bs, seq_s, _ = x_sample.shape
    depth = w_in.shape[0]
    assert depth == 1 and seq_s == 1
    n_meta = meta_tokens.shape[0]
    _, _, heads_dn, dk, dv = state_dn.shape
    _, _, heads_rw, n_rw, _ = state_rw.shape
    n_conv = dn_conv_w.shape[2]
    nqkv = 2 * heads_dn * dk + heads_dn * dv
    assert n_conv == nqkv == 1536 and dn_conv_w.shape[1] == 4 and heads_dn * dv == 512
    nw = heads_rw * n_rw
    rank_w, rank_a, rank_g = rw_w2.shape[1], rw_a2.shape[1], rw_g2.shape[1]
    assert nw == 512 and rank_w + rank_a == LANES and rank_g == LANES
    n_rwp = 3 * nw + rank_w + rank_a + rank_g
    o_z = nqkv
    o_beta = o_z + heads_dn * dv
    o_alpha = o_beta + heads_dn
    o_rw = o_alpha + heads_dn
    o_gate = o_rw + n_rwp
    assert w_in.shape[2] == o_gate + 2 * d and 2 * heads_dn == 8

    wi = w_in[0]
    seg1 = jnp.concatenate([wi[:, o_rw:o_gate], wi[:, o_beta:o_rw],
                            jnp.zeros((d, 2048 - n_rwp - 2 * heads_dn), F32)], axis=1)
    w_packed = jnp.concatenate([wi[:, 0:o_beta], seg1, wi[:, o_gate:]], axis=1).astype(BF16)
    w_bt = wi[:, o_beta:o_rw].T.astype(BF16)
    g1 = ln1[0][None, :]
    cw = dn_conv_w[0]
    prow = jnp.stack([_pad_lanes(dn_a_log[0], LANES, heads_dn), _pad_lanes(dn_dt_bias[0], LANES, heads_dn)])
    pcol = jnp.stack([_pad_lanes(dn_a_log[0], 8, heads_dn), _pad_lanes(dn_dt_bias[0], 8, heads_dn)], axis=1)
    nwd = dn_norm_w[0][None, :]
    mu = rw_mu[0]
    par = jnp.stack([mu[0:nw], mu[nw:2 * nw], mu[2 * nw:3 * nw], _pad_lanes(mu[3 * nw:], nw),
                     rw_w0[0], rw_a0[0], rw_k_k[0], rw_k_a[0]])
    wwa = jnp.zeros((LANES, 2 * nw), F32)
    wwa = wwa.at[0:rank_w, 0:nw].set(rw_w2[0]).at[rank_w:, nw:].set(rw_a2[0]).astype(BF16)
    g2 = rw_g2[0].astype(BF16)
    grp = jnp.arange(nw) // n_rw
    bd = (grp[:, None] == grp[None, :]).astype(BF16)
    epi = jnp.zeros((8, nw), F32).at[0].set(rw_gn_w[0]).at[1].set(rw_gn_b[0]).at[2].set(rw_r_k[0].reshape(-1))
    wd = w_up_dn[0].astype(BF16)
    wr = w_up_rw[0].astype(BF16)
    wo = w_out[0].astype(BF16)
    l2 = ln2[0][None, :]
    wq = peer_wq[0].astype(BF16)
    pk = peer_keys[0]
    keys = pk.reshape(pk.shape[0] * pk.shape[1], pk.shape[2], pk.shape[3]).astype(BF16)
    pu = peer_u[0].astype(BF16)
    pvt = peer_v[0].T.astype(BF16)
    lf = ln_f[None, :]

    assert n_meta <= ROWS and seq % ROWS == 0
    nreal = seq // ROWS
    x_meta = jnp.concatenate([jnp.zeros((ROWS - n_meta, d), F32), meta_tokens.astype(F32)], axis=0)
    xr = x_prompt.reshape(bp * seq, d)
    tm = max(t for t in (1024, 512, 256, ROWS) if (bp * seq) % t == 0)
    pm, batm = _norm_proj(x_meta, g1, w_packed, w_bt, ROWS)
    pp, batp = _norm_proj(xr, g1, w_packed, w_bt, tm)
    odn_p, p_dn = _dn_prompt(pp, batp, pm, batm, cw, prow, pcol, nwd, bp, nreal, heads_dn, dk, dv)
    orw_p, p_rw = _rw_prompt(pp, pm, par, wwa, g2, bd, epi, bp, nreal, heads_rw, n_rw, rank_w)
    tmm = max(t for t in (512, 256, ROWS) if (bp * seq) % t == 0)
    hp = _merge(xr, odn_p, orw_p, pp, wd, wr, wo, 1, bp * seq // tmm, bp * seq // tmm, 0, tmm)
    y_prompt = _peer(hp, l2, wq, keys, pu, pvt, lf, 512).reshape(bp, seq, d)
    pp3 = pp.reshape(bp, seq, -1)
    p_conv = pp3[:, seq - 3:, 0:nqkv]
    p_shift = pp3[:, seq - 1:, 2048:2048 + n_rwp]

    xs = x_sample.reshape(bs, d)
    ps, _ = _norm_proj(xs, g1, w_packed, w_bt, bs)
    cache_t = jnp.moveaxis(cache_dn_conv[0], 1, 0)
    odn_s, s_dn = _dn_sample(ps, cache_t, cw, prow, nwd, state_dn[0], heads_dn, dk, dv)
    orw_s, s_rw = _rw_sample(ps, cache_rw_shift[0][:, 0, :], par, wwa, g2, bd, epi, state_rw[0],
                             heads_rw, n_rw, rank_w)
    hs = _merge(xs, odn_s, orw_s, ps, wd, wr, wo, 1, 1, 1, 0, bs)
    y_sample = _peer(hs, l2, wq, keys, pu, pvt, lf, bs).reshape(bs, 1, d)
    s_conv = jnp.concatenate([cache_dn_conv[0][:, 1:], ps[:, None, 0:nqkv]], axis=1)
    s_shift = ps[:, None, 2048:2048 + n_rwp]

    return (y_prompt, y_sample, p_conv[None], p_dn[None], p_shift[None], p_rw[None],
            s_conv[None], s_dn[None], s_shift[None], s_rw[None])
```

```python
import functools

import jax
import jax.numpy as jnp
from jax import lax
from jax.experimental import pallas as pl
from jax.experimental.pallas import tpu as pltpu

F32 = jnp.float32
BF16 = jnp.bfloat16

NORM_EPS = 1e-6
L2_EPS = 1e-6
GN_EPS = 64e-5
PEER_TOPK = 16

LANES = 128
CHUNK = 64
ROWS = 4 * CHUNK
INV_BLOCK = 16
VMEM_LIMIT = 52 * 1024 * 1024

NN = ((1,), (0,))
NT = ((1,), (1,))
TN = ((0,), (0,))


def _dg(a, b, dims=NN):
    return lax.dot_general(a, b, (dims, ((), ())), preferred_element_type=F32)


def _mm1(a, b, dims=NN):
    return _dg(a.astype(BF16), b.astype(BF16), dims)


def _hi_lo(x):
    hi = x.astype(BF16)
    lo = (x - hi.astype(F32)).astype(BF16)
    return hi, lo


def _mm3(a, b, dims=NN):
    a1, a2 = _hi_lo(a)
    b1, b2 = _hi_lo(b)
    return _dg(a1, b1, dims) + (_dg(a1, b2, dims) + _dg(a2, b1, dims))


def _split3(x):
    x1 = x.astype(BF16)
    r1 = x - x1.astype(F32)
    x2 = r1.astype(BF16)
    x3 = (r1 - x2.astype(F32)).astype(BF16)
    return x1, x2, x3


def _mm_mask_lhs(m, x, dims=NN):
    x1, x2, x3 = _split3(x)
    m = m.astype(BF16)
    return _dg(m, x1, dims) + (_dg(m, x2, dims) + _dg(m, x3, dims))


def _mm_mask_rhs(x, m, dims=NN):
    x1, x2, x3 = _split3(x)
    m = m.astype(BF16)
    return _dg(x1, m, dims) + (_dg(x2, m, dims) + _dg(x3, m, dims))


def _sigmoid(x):
    return jax.nn.sigmoid(x)


def _silu(x):
    return x * jax.nn.sigmoid(x)


def _softplus(x):
    return jnp.maximum(x, 0.0) + jnp.log(1.0 + jnp.exp(-jnp.abs(x)))


def _iota2(n, m):
    return (lax.broadcasted_iota(jnp.int32, (n, m), 0), lax.broadcasted_iota(jnp.int32, (n, m), 1))


def _mm3_each(pairs, dims=NN):
    split = [(_hi_lo(a), _hi_lo(b)) for a, b in pairs]
    return [_dg(a1, b1, dims) + (_dg(a1, b2, dims) + _dg(a2, b1, dims)) for (a1, a2), (b1, b2) in split]


def _mm1_each(pairs, dims=NN):
    cast = [(a.astype(BF16), b.astype(BF16)) for a, b in pairs]
    return [_dg(a, b, dims) for a, b in cast]


def _tri_inv_each(mats, ri, ci):
    assert mats[0].shape == (CHUNK, CHUNK) and CHUNK == 4 * INV_BLOCK and INV_BLOCK == 16
    eye = jnp.where(ri == ci, 1.0, 0.0).astype(F32)
    same = lax.shift_right_logical(ri, 4) == lax.shift_right_logical(ci, 4)
    n = [jnp.where(same, a, 0.0) for a in mats]
    e = [a - x for a, x in zip(mats, n)]
    n2 = _mm1_each(list(zip(n, n)))
    n4 = _mm1_each(list(zip(n2, n2)))
    n8 = _mm1_each(list(zip(n4, n4)))
    d = _mm1_each([(eye - x, eye + y) for x, y in zip(n, n2)])
    d = _mm1_each([(x, eye + y) for x, y in zip(d, n4)])
    d = _mm1_each([(x, eye + y) for x, y in zip(d, n8)])
    f = _mm1_each(list(zip(d, e)))
    f2 = _mm1_each(list(zip(f, f)))
    x = _mm1_each([(eye - p, eye + q) for p, q in zip(f, f2)])
    return _mm1_each(list(zip(x, d)))


def _chunk_masks():
    ri, ci = _iota2(ROWS, ROWS)
    same = lax.shift_right_logical(ri, 6) == lax.shift_right_logical(ci, 6)
    low = jnp.where(same, jnp.where(ci <= ri, 1.0, 0.0), 0.0).astype(BF16)
    up = jnp.where(same, jnp.where(ri <= ci, 1.0, 0.0), 0.0).astype(BF16)
    return low, up


def _norm_proj_kernel(x_ref, g_ref, w_ref, wbt_ref, o_ref, bt_ref, xn_ref):
    @pl.when(pl.program_id(1) == 0)
    def _():
        x = x_ref[...]
        xn = x * lax.rsqrt(jnp.mean(x * x, -1, keepdims=True) + NORM_EPS) * g_ref[...]
        xn_ref[...] = xn.astype(BF16)
        bt_ref[...] = _dg(wbt_ref[...], xn_ref[...], NT)
    o_ref[...] = _dg(xn_ref[...], w_ref[...])


def _norm_proj(x, g, w_packed, w_bt, tm):
    m, d = x.shape
    n = w_packed.shape[1]
    tn = 2048
    return pl.pallas_call(
        _norm_proj_kernel,
        grid=(m // tm, n // tn),
        in_specs=[
            pl.BlockSpec((tm, d), lambda i, j: (i, 0)),
            pl.BlockSpec((1, d), lambda i, j: (0, 0)),
            pl.BlockSpec((d, tn), lambda i, j: (0, j)),
            pl.BlockSpec((8, d), lambda i, j: (0, 0)),
        ],
        out_specs=[
            pl.BlockSpec((tm, tn), lambda i, j: (i, j)),
            pl.BlockSpec((8, tm), lambda i, j: (0, i)),
        ],
        out_shape=[jax.ShapeDtypeStruct((m, n), F32), jax.ShapeDtypeStruct((8, m), F32)],
        scratch_shapes=[pltpu.VMEM((tm, d), BF16)],
        compiler_params=pltpu.CompilerParams(
            dimension_semantics=("arbitrary", "arbitrary"), vmem_limit_bytes=VMEM_LIMIT),
        name="norm_proj",
    )(x, g, w_packed, w_bt)


def _l2norm_rows(x):
    return x * lax.rsqrt(jnp.sum(x * x, -1, keepdims=True) + L2_EPS)


def _dn_out_norm(o, z, nw):
    on = o * lax.rsqrt(jnp.mean(o * o, -1, keepdims=True) + NORM_EPS) * nw
    return on * _silu(z)


def _dn_prompt_kernel(p0_ref, ba_ref, bat_ref, pm0_ref, bam_ref, batm_ref, cw_ref, prow_ref, pcol_ref, nw_ref,
                      o_ref, s_out_ref, s_ref, cbuf_ref, *, heads, dk, dv):
    c = pl.program_id(1)
    nqk = heads * dk
    nconv = 2 * nqk + heads * dv

    @pl.when(c == 0)
    def _():
        s_ref[...] = jnp.zeros_like(s_ref)
        cbuf_ref[0:8, :] = jnp.zeros((8, nconv), F32)

    first = c == 0
    p0 = jnp.where(first, pm0_ref[...], p0_ref[...])
    u = p0[:, 0:nconv]
    cbuf_ref[8:8 + ROWS, :] = u
    y = (cw_ref[3:4, :] * u + cw_ref[2:3, :] * cbuf_ref[7:7 + ROWS, :]
         + cw_ref[1:2, :] * cbuf_ref[6:6 + ROWS, :] + cw_ref[0:1, :] * cbuf_ref[5:5 + ROWS, :])
    cbuf_ref[0:8, :] = u[ROWS - 8:ROWS, :]
    qkv = _silu(y)

    ba = jnp.where(first, bam_ref[...], ba_ref[...])
    beta_all = _sigmoid(ba)
    g_col = -jnp.exp(prow_ref[0:1, :]) * _softplus(ba + prow_ref[1:2, :])
    bat = jnp.where(first, batm_ref[...], bat_ref[...])
    g_row = -jnp.exp(pcol_ref[:, 0:1]) * _softplus(bat + pcol_ref[:, 1:2])
    low, up = _chunk_masks()
    gc_all = _mm_mask_lhs(low, g_col)
    gr_all = _mm_mask_rhs(g_row, up)
    ri, ci = _iota2(CHUNK, CHUNK)
    scale = dk ** -0.5

    hcs = [(s, h) for s in range(ROWS // CHUNK) for h in range(heads)]
    q_l, k_l, kb_l, vb_l, gc_l, dec_l = [], [], [], [], [], []
    for s, h in hcs:
        r0 = s * CHUNK
        q_l.append(_l2norm_rows(qkv[r0:r0 + CHUNK, h * dk:(h + 1) * dk]) * scale)
        k = _l2norm_rows(qkv[r0:r0 + CHUNK, nqk + h * dk:nqk + (h + 1) * dk])
        beta = beta_all[r0:r0 + CHUNK, h:h + 1]
        gc = gc_all[r0:r0 + CHUNK, heads + h:heads + h + 1]
        gr = gr_all[heads + h:heads + h + 1, r0:r0 + CHUNK]
        k_l.append(k)
        kb_l.append(k * beta)
        vb_l.append(qkv[r0:r0 + CHUNK, 2 * nqk + h * dv:2 * nqk + (h + 1) * dv] * beta)
        gc_l.append(gc)
        dec_l.append(jnp.where(ci <= ri, jnp.exp(jnp.minimum(gc - gr, 0.0)), 0.0))
    a_l = [jnp.where(ci < ri, x * dec, 0.0)
           for x, dec in zip(_mm1_each(list(zip(kb_l, k_l)), NT), dec_l)]
    t_l = _tri_inv_each(a_l, ri, ci)
    eg_l = [jnp.exp(gc) for gc in gc_l]
    uu_l = _mm1_each(list(zip(t_l, vb_l)))
    ww_l = _mm1_each([(t, kb * eg) for t, kb, eg in zip(t_l, kb_l, eg_l)])
    qk_l = [jnp.where(ci <= ri, x * dec, 0.0)
            for x, dec in zip(_mm1_each(list(zip(q_l, k_l)), NT), dec_l)]

    for s in range(ROWS // CHUNK):
        r0 = s * CHUNK
        ix = [s * heads + h for h in range(heads)]
        st_l = [s_ref[h] for h in range(heads)]
        ws_l = _mm1_each([(ww_l[i], st) for i, st in zip(ix, st_l)])
        qs_l = _mm1_each([(q_l[i] * eg_l[i], st) for i, st in zip(ix, st_l)])
        vn_l = [uu_l[i] - ws for i, ws in zip(ix, ws_l)]
        qv_l = _mm1_each([(qk_l[i], vn) for i, vn in zip(ix, vn_l)])
        gl_l = [gc_l[i][CHUNK - 1:CHUNK, :] for i in ix]
        kv_l = _mm1_each([(k_l[i] * jnp.exp(gl - gc_l[i]), vn) for i, gl, vn in zip(ix, gl_l, vn_l)], TN)
        for h in range(heads):
            s_ref[h] = st_l[h] * jnp.exp(gl_l[h]) + kv_l[h]
            z = p0[r0:r0 + CHUNK, nconv + h * dv:nconv + (h + 1) * dv]
            o_ref[r0:r0 + CHUNK, h * dv:(h + 1) * dv] = _dn_out_norm(
                qs_l[h] + qv_l[h], z, nw_ref[...]).astype(o_ref.dtype)

    @pl.when(c == pl.num_programs(1) - 1)
    def _():
        s_out_ref[0] = s_ref[...]


def _prompt_block(nreal):
    return lambda b, c: b * nreal + jnp.maximum(c - 1, 0)


def _dn_prompt(p, bat, pm, batm, cw, prow, pcol, nw, batch, nreal, heads, dk, dv):
    m = p.shape[0]
    nconv = 2 * heads * dk + heads * dv
    ba_blk = (2048 + 1792) // LANES
    kern = functools.partial(_dn_prompt_kernel, heads=heads, dk=dk, dv=dv)
    blk = _prompt_block(nreal)
    return pl.pallas_call(
        kern,
        grid=(batch, nreal + 1),
        in_specs=[
            pl.BlockSpec((ROWS, 2048), lambda b, c: (blk(b, c), 0)),
            pl.BlockSpec((ROWS, LANES), lambda b, c: (blk(b, c), ba_blk)),
            pl.BlockSpec((8, ROWS), lambda b, c: (0, blk(b, c))),
            pl.BlockSpec((ROWS, 2048), lambda b, c: (0, 0)),
            pl.BlockSpec((ROWS, LANES), lambda b, c: (0, ba_blk)),
            pl.BlockSpec((8, ROWS), lambda b, c: (0, 0)),
            pl.BlockSpec((4, nconv), lambda b, c: (0, 0)),
            pl.BlockSpec((2, LANES), lambda b, c: (0, 0)),
            pl.BlockSpec((8, 2), lambda b, c: (0, 0)),
            pl.BlockSpec((1, dv), lambda b, c: (0, 0)),
        ],
        out_specs=[
            pl.BlockSpec((ROWS, heads * dv), lambda b, c: (blk(b, c), 0)),
            pl.BlockSpec((1, heads, dk, dv), lambda b, c: (b, 0, 0, 0)),
        ],
        out_shape=[jax.ShapeDtypeStruct((m, heads * dv), BF16),
                   jax.ShapeDtypeStruct((batch, heads, dk, dv), F32)],
        scratch_shapes=[pltpu.VMEM((heads, dk, dv), F32), pltpu.VMEM((8 + ROWS, nconv), F32)],
        compiler_params=pltpu.CompilerParams(
            dimension_semantics=("arbitrary", "arbitrary"), vmem_limit_bytes=VMEM_LIMIT),
        name="dn_prompt",
    )(p, p, bat, pm, pm, batm, cw, prow, pcol, nw)


def _col_from_row(row, eye):
    n = row.shape[1]
    return jnp.sum(jnp.where(eye, jnp.broadcast_to(row, (n, n)), 0.0), axis=1, keepdims=True)


def _row_from_col(col, eye):
    n = col.shape[0]
    return jnp.sum(jnp.where(eye, jnp.broadcast_to(col, (n, n)), 0.0), axis=0, keepdims=True)


def _dn_sample_kernel(p0_ref, ba_ref, cache_ref, cw_ref, prow_ref, nw_ref, s_in_ref,
                      o_ref, s_out_ref, q_s, k_s, v_s, g_s, b_s, o_s, *, heads, dk, dv, tb):
    nqk = heads * dk
    nconv = 2 * nqk + heads * dv
    u = p0_ref[:, 0:nconv]
    y = (cw_ref[3:4, :] * u + cw_ref[2:3, :] * cache_ref[2] + cw_ref[1:2, :] * cache_ref[1]
         + cw_ref[0:1, :] * cache_ref[0])
    qkv = _silu(y)
    ba = ba_ref[...]
    b_s[...] = _sigmoid(ba)
    g_s[...] = -jnp.exp(prow_ref[0:1, :]) * _softplus(ba + prow_ref[1:2, :])
    scale = dk ** -0.5
    for h in range(heads):
        q_s[:, h * dk:(h + 1) * dk] = _l2norm_rows(qkv[:, h * dk:(h + 1) * dk]) * scale
        k_s[:, h * dk:(h + 1) * dk] = _l2norm_rows(qkv[:, nqk + h * dk:nqk + (h + 1) * dk])
    v_s[...] = qkv[:, 2 * nqk:]
    ri, ci = _iota2(dk, dk)
    eye = ri == ci

    for i in range(tb):
        for h in range(heads):
            k_row = k_s[i:i + 1, h * dk:(h + 1) * dk]
            q_row = q_s[i:i + 1, h * dk:(h + 1) * dk]
            v_row = v_s[i:i + 1, h * dv:(h + 1) * dv]
            g = g_s[i:i + 1, heads + h:heads + h + 1]
            beta = b_s[i:i + 1, h:h + 1]
            k_col = _col_from_row(k_row, eye)
            q_col = _col_from_row(q_row, eye)
            st = s_in_ref[i, h] * jnp.exp(g)
            kv = jnp.sum(st * k_col, axis=0, keepdims=True)
            delta = (v_row - kv) * beta
            st = st + k_col * delta
            s_out_ref[i, h] = st
            o_s[i:i + 1, h * dv:(h + 1) * dv] = jnp.sum(st * q_col, axis=0, keepdims=True)
    for h in range(heads):
        z = p0_ref[:, nconv + h * dv:nconv + (h + 1) * dv]
        o_ref[:, h * dv:(h + 1) * dv] = _dn_out_norm(
            o_s[:, h * dv:(h + 1) * dv], z, nw_ref[...]).astype(o_ref.dtype)


def _dn_sample(p, cache_t, cw, prow, nw, state, heads, dk, dv, tb=8):
    m = p.shape[0]
    nconv = 2 * heads * dk + heads * dv
    ba_blk = (2048 + 1792) // LANES
    kern = functools.partial(_dn_sample_kernel, heads=heads, dk=dk, dv=dv, tb=tb)
    return pl.pallas_call(
        kern,
        grid=(m // tb,),
        in_specs=[
            pl.BlockSpec((tb, 2048), lambda i: (i, 0)),
            pl.BlockSpec((tb, LANES), lambda i: (i, ba_blk)),
            pl.BlockSpec((3, tb, nconv), lambda i: (0, i, 0)),
            pl.BlockSpec((4, nconv), lambda i: (0, 0)),
            pl.BlockSpec((2, LANES), lambda i: (0, 0)),
            pl.BlockSpec((1, dv), lambda i: (0, 0)),
            pl.BlockSpec((tb, heads, dk, dv), lambda i: (i, 0, 0, 0)),
        ],
        out_specs=[
            pl.BlockSpec((tb, heads * dv), lambda i: (i, 0)),
            pl.BlockSpec((tb, heads, dk, dv), lambda i: (i, 0, 0, 0)),
        ],
        out_shape=[jax.ShapeDtypeStruct((m, heads * dv), BF16),
                   jax.ShapeDtypeStruct(state.shape, F32)],
        scratch_shapes=[pltpu.VMEM((tb, heads * dk), F32), pltpu.VMEM((tb, heads * dk), F32),
                        pltpu.VMEM((tb, heads * dv), F32), pltpu.VMEM((tb, LANES), F32),
                        pltpu.VMEM((tb, LANES), F32), pltpu.VMEM((tb, heads * dv), F32)],
        compiler_params=pltpu.CompilerParams(
            dimension_semantics=("arbitrary",), vmem_limit_bytes=VMEM_LIMIT),
        name="dn_sample",
    )(p, p, cache_t, cw, prow, nw, state)


def _group_sum(x, bd):
    return _mm_mask_rhs(x, bd)


def _rw_prep(prw, prev, par_ref, wwa_ref, g2_ref, bd, nw, rank_w):
    r = prw[:, 0:nw] + (prev[:, 0:nw] - prw[:, 0:nw]) * par_ref[0:1, :]
    kr = prw[:, nw:2 * nw] + (prev[:, nw:2 * nw] - prw[:, nw:2 * nw]) * par_ref[1:2, :]
    vr = prw[:, 2 * nw:3 * nw] + (prev[:, 2 * nw:3 * nw] - prw[:, 2 * nw:3 * nw]) * par_ref[2:3, :]
    c0 = 3 * nw
    lo = prw[:, c0:c0 + 2 * LANES]
    lo = lo + (prev[:, c0:c0 + 2 * LANES] - lo) * par_ref[3:4, 0:2 * LANES]
    wa = lo[:, 0:LANES]
    lane = lax.broadcasted_iota(jnp.int32, wa.shape, 1)
    wa = jnp.where(lane < rank_w, jnp.tanh(wa), wa)
    proj = _mm1(wa, wwa_ref[...])
    w_log = -_softplus(-(par_ref[4:5, :] + proj[:, 0:nw])) - 0.5
    logdec = -jnp.exp(w_log)
    a = _sigmoid(par_ref[5:6, :] + proj[:, nw:2 * nw])
    gate = _mm1(_sigmoid(lo[:, LANES:2 * LANES]), g2_ref[...])
    kk = kr * par_ref[6:7, :]
    kk = kk * lax.rsqrt(_group_sum(kk * kk, bd) + L2_EPS)
    k2 = kr * (1.0 + (a - 1.0) * par_ref[7:8, :])
    return r, logdec, k2, vr, -kk, kk * a, gate


def _rw_epilogue(y, r, k2, v, gate, epi_ref, bd, n):
    mu = _group_sum(y, bd) * (1.0 / n)
    d = y - mu
    var = _group_sum(d * d, bd) * (1.0 / n)
    yn = d * lax.rsqrt(var + GN_EPS) * epi_ref[0:1, :] + epi_ref[1:2, :]
    bonus = _group_sum(r * k2 * epi_ref[2:3, :], bd) * v
    return (yn + bonus) * gate


def _rw_prompt_kernel(p1_ref, pm1_ref, par_ref, wwa_ref, g2_ref, bd_ref, epi_ref,
                      o_ref, s_out_ref, s_ref, sbuf_ref, y_s, *, heads, n, rank_w):
    c = pl.program_id(1)
    nw = heads * n
    nrw = 3 * nw + 2 * LANES

    @pl.when(c == 0)
    def _():
        s_ref[...] = jnp.zeros_like(s_ref)
        sbuf_ref[0:8, :] = jnp.zeros((8, nrw), F32)

    prw = jnp.where(c == 0, pm1_ref[:, 0:nrw], p1_ref[:, 0:nrw])
    sbuf_ref[8:8 + ROWS, :] = prw
    prev = sbuf_ref[7:7 + ROWS, :]
    sbuf_ref[0:8, :] = prw[ROWS - 8:ROWS, :]
    bd = bd_ref[...]
    r, logdec, k2, v, a_in, b_in, gate = _rw_prep(prw, prev, par_ref, wwa_ref, g2_ref, bd, nw, rank_w)
    low, _ = _chunk_masks()
    g_all = _mm_mask_lhs(low, logdec)
    ri, ci = _iota2(CHUNK, CHUNK)

    hcs = [(s, h) for s in range(ROWS // CHUNK) for h in range(heads)]
    at_l, rt_l, bt_l, kt_l, bd_l, kd_l, vv_l, egl_l = [], [], [], [], [], [], [], []
    for s, h in hcs:
        sl = (slice(s * CHUNK, (s + 1) * CHUNK), slice(h * n, (h + 1) * n))
        g = g_all[sl]
        eg = jnp.exp(g)
        ieg = jnp.exp(-g)
        gl = g[CHUNK - 1:CHUNK, :]
        tail = jnp.exp(gl - g)
        at_l.append(a_in[sl] * jnp.exp(g - logdec[sl]))
        rt_l.append(r[sl] * eg)
        bt_l.append(b_in[sl] * ieg)
        kt_l.append(k2[sl] * ieg)
        bd_l.append(b_in[sl] * tail)
        kd_l.append(k2[sl] * tail)
        vv_l.append(v[sl])
        egl_l.append(jnp.exp(gl))
    nab_l = [jnp.where(ci < ri, -x, 0.0) for x in _mm1_each(list(zip(at_l, bt_l)), NT)]
    aak_l = [jnp.where(ci < ri, x, 0.0) for x in _mm1_each(list(zip(at_l, kt_l)), NT)]
    rb_l = [jnp.where(ci <= ri, x, 0.0) for x in _mm1_each(list(zip(rt_l, bt_l)), NT)]
    rk_l = [jnp.where(ci <= ri, x, 0.0) for x in _mm1_each(list(zip(rt_l, kt_l)), NT)]
    t_l = _tri_inv_each(nab_l, ri, ci)
    akv_l = _mm1_each(list(zip(aak_l, vv_l)))
    rkv_l = _mm1_each(list(zip(rk_l, vv_l)))
    vkd_l = _mm1_each(list(zip(vv_l, kd_l)), TN)

    for s in range(ROWS // CHUNK):
        r0 = s * CHUNK
        ix = [s * heads + h for h in range(heads)]
        st_l = [s_ref[h] for h in range(heads)]
        as_l = _mm1_each([(at_l[i], st) for i, st in zip(ix, st_l)], NT)
        rs_l = _mm1_each([(rt_l[i], st) for i, st in zip(ix, st_l)], NT)
        uu_l = _mm1_each([(t_l[i], x + akv_l[i]) for i, x in zip(ix, as_l)])
        ru_l = _mm1_each([(rb_l[i], uu) for i, uu in zip(ix, uu_l)])
        ub_l = _mm1_each([(uu, bd_l[i]) for i, uu in zip(ix, uu_l)], TN)
        for h in range(heads):
            i = ix[h]
            y_s[r0:r0 + CHUNK, h * n:(h + 1) * n] = rs_l[h] + ru_l[h] + rkv_l[i]
            s_ref[h] = st_l[h] * egl_l[i] + ub_l[h] + vkd_l[i]

    o_ref[...] = _rw_epilogue(y_s[...], r, k2, v, gate, epi_ref, bd, n).astype(o_ref.dtype)

    @pl.when(c == pl.num_programs(1) - 1)
    def _():
        s_out_ref[0] = s_ref[...]


def _rw_prompt(p, pm, par, wwa, g2, bd, epi, batch, nreal, heads, n, rank_w):
    m = p.shape[0]
    nw = heads * n
    nrw = 3 * nw + 2 * LANES
    kern = functools.partial(_rw_prompt_kernel, heads=heads, n=n, rank_w=rank_w)
    blk = _prompt_block(nreal)
    return pl.pallas_call(
        kern,
        grid=(batch, nreal + 1),
        in_specs=[
            pl.BlockSpec((ROWS, 2048), lambda b, c: (blk(b, c), 1)),
            pl.BlockSpec((ROWS, 2048), lambda b, c: (0, 1)),
            pl.BlockSpec((8, nw), lambda b, c: (0, 0)),
            pl.BlockSpec((LANES, 2 * nw), lambda b, c: (0, 0)),
            pl.BlockSpec((LANES, nw), lambda b, c: (0, 0)),
            pl.BlockSpec((nw, nw), lambda b, c: (0, 0)),
            pl.BlockSpec((8, nw), lambda b, c: (0, 0)),
        ],
        out_specs=[
            pl.BlockSpec((ROWS, nw), lambda b, c: (blk(b, c), 0)),
            pl.BlockSpec((1, heads, n, n), lambda b, c: (b, 0, 0, 0)),
        ],
        out_shape=[jax.ShapeDtypeStruct((m, nw), BF16),
                   jax.ShapeDtypeStruct((batch, heads, n, n), F32)],
        scratch_shapes=[pltpu.VMEM((heads, n, n), F32), pltpu.VMEM((8 + ROWS, nrw), F32),
                        pltpu.VMEM((ROWS, nw), F32)],
        compiler_params=pltpu.CompilerParams(
            dimension_semantics=("arbitrary", "arbitrary"), vmem_limit_bytes=VMEM_LIMIT),
        name="rw_prompt",
    )(p, pm, par, wwa, g2, bd, epi)


def _rw_sample_kernel(p1_ref, prev_ref, par_ref, wwa_ref, g2_ref, bd_ref, epi_ref, s_in_ref,
                      o_ref, s_out_ref, r_s, w_s, k_s, v_s, a_s, b_s, y_s, *, heads, n, rank_w, tb):
    nw = heads * n
    nrw = 3 * nw + 2 * LANES
    bd = bd_ref[...]
    prw = p1_ref[:, 0:nrw]
    r, logdec, k2, v, a_in, b_in, gate = _rw_prep(prw, prev_ref[...], par_ref, wwa_ref, g2_ref, bd, nw, rank_w)
    r_s[...] = r
    w_s[...] = jnp.exp(logdec)
    k_s[...] = k2
    v_s[...] = v
    a_s[...] = a_in
    b_s[...] = b_in
    ri, ci = _iota2(n, n)
    eye = ri == ci

    group = 8
    for i0 in range(0, tb, group):
        items = [(i, h) for i in range(i0, i0 + group) for h in range(heads)]
        sls = [(slice(i, i + 1), slice(h * n, (h + 1) * n)) for i, h in items]
        st_l = [s_in_ref[i, h] for i, h in items]
        sa_l = [jnp.sum(st * a_s[sl], axis=1, keepdims=True) for st, sl in zip(st_l, sls)]
        vc_l = [_col_from_row(v_s[sl], eye) for sl in sls]
        st_l = [st * w_s[sl] + sa * b_s[sl] + vc * k_s[sl] for st, sa, vc, sl in zip(st_l, sa_l, vc_l, sls)]
        yc_l = [jnp.sum(st * r_s[sl], axis=1, keepdims=True) for st, sl in zip(st_l, sls)]
        for (i, h), sl, st, yc in zip(items, sls, st_l, yc_l):
            s_out_ref[i, h] = st
            y_s[sl] = _row_from_col(yc, eye)
    o_ref[...] = _rw_epilogue(y_s[...], r, k2, v, gate, epi_ref, bd, n).astype(o_ref.dtype)


def _rw_sample(p, prev, par, wwa, g2, bd, epi, state, heads, n, rank_w, tb=8):
    m = p.shape[0]
    nw = heads * n
    nrw = 3 * nw + 2 * LANES
    kern = functools.partial(_rw_sample_kernel, heads=heads, n=n, rank_w=rank_w, tb=tb)
    return pl.pallas_call(
        kern,
        grid=(m // tb,),
        in_specs=[
            pl.BlockSpec((tb, 2048), lambda i: (i, 1)),
            pl.BlockSpec((tb, nrw), lambda i: (i, 0)),
            pl.BlockSpec((8, nw), lambda i: (0, 0)),
            pl.BlockSpec((LANES, 2 * nw), lambda i: (0, 0)),
            pl.BlockSpec((LANES, nw), lambda i: (0, 0)),
            pl.BlockSpec((nw, nw), lambda i: (0, 0)),
            pl.BlockSpec((8, nw), lambda i: (0, 0)),
            pl.BlockSpec((tb, heads, n, n), lambda i: (i, 0, 0, 0)),
        ],
        out_specs=[
            pl.BlockSpec((tb, nw), lambda i: (i, 0)),
            pl.BlockSpec((tb, heads, n, n), lambda i: (i, 0, 0, 0)),
        ],
        out_shape=[jax.ShapeDtypeStruct((m, nw), BF16), jax.ShapeDtypeStruct(state.shape, F32)],
        scratch_shapes=[pltpu.VMEM((tb, nw), F32)] * 7,
        compiler_params=pltpu.CompilerParams(
            dimension_semantics=("arbitrary",), vmem_limit_bytes=VMEM_LIMIT),
        name="rw_sample",
    )(p, prev, par, wwa, g2, bd, epi, state)


def _merge_kernel(x_ref, odn_ref, orw_ref, gt_ref, wd_ref, wr_ref, wo_ref, h_ref, *, d):
    ga = _sigmoid(gt_ref[:, 0:d])
    gb = _sigmoid(gt_ref[:, d:2 * d])
    m = ga * _dg(odn_ref[...], wd_ref[...]) + gb * _dg(orw_ref[...], wr_ref[...])
    h_ref[...] = x_ref[...] + _dg(m.astype(BF16), wo_ref[...])


def _merge(x, odn, orw, p, wd, wr, wo, batch, nblk_in, nblk_out, skip, tm):
    d = x.shape[1]
    nd = odn.shape[1]
    nr = orw.shape[1]
    kern = functools.partial(_merge_kernel, d=d)
    in_row = lambda b, i: (b * nblk_in + skip + i, 0)
    return pl.pallas_call(
        kern,
        grid=(batch, nblk_out),
        in_specs=[
            pl.BlockSpec((tm, d), in_row),
            pl.BlockSpec((tm, nd), in_row),
            pl.BlockSpec((tm, nr), in_row),
            pl.BlockSpec((tm, 2 * d), lambda b, i: (b * nblk_in + skip + i, 2)),
            pl.BlockSpec((nd, d), lambda b, i: (0, 0)),
            pl.BlockSpec((nr, d), lambda b, i: (0, 0)),
            pl.BlockSpec((d, d), lambda b, i: (0, 0)),
        ],
        out_specs=pl.BlockSpec((tm, d), lambda b, i: (b * nblk_out + i, 0)),
        out_shape=jax.ShapeDtypeStruct((batch * nblk_out * tm, d), F32),
        compiler_params=pltpu.CompilerParams(
            dimension_semantics=("arbitrary", "arbitrary"), vmem_limit_bytes=VMEM_LIMIT),
        name="merge",
    )(x, odn, orw, p, wd, wr, wo)


def _top_values_each(xs, k, ranked):
    rows = [[] for _ in xs]
    cur = list(xs)
    rank = [jnp.full(x.shape, float(k), F32) if rk else None for x, rk in zip(xs, ranked)]
    for r in range(k):
        top = [jnp.max(c, axis=0, keepdims=True) for c in cur]
        hit = [c == m for c, m in zip(cur, top)]
        for i, m in enumerate(top):
            rows[i].append(m)
            if ranked[i]:
                rank[i] = jnp.where(hit[i], float(r), rank[i])
        cur = [jnp.where(h, -jnp.inf, c) for h, c in zip(hit, cur)]
    return [jnp.concatenate(r, axis=0) for r in rows], rank


def _peer_kernel(h_ref, ln2_ref, wq_ref, keys_ref, u_ref, vt_ref, lnf_ref, y_ref,
                 xn_s, q_s, e0_s, e1_s, tau_s, ht_s, g_s, acc_s, *, heads, nkeys, eb, ne):
    e = pl.program_id(1)
    k = PEER_TOPK

    @pl.when(e == 0)
    def _():
        x = h_ref[...]
        xn = x * lax.rsqrt(jnp.mean(x * x, -1, keepdims=True) + NORM_EPS) * ln2_ref[...]
        xn_s[...] = xn.astype(BF16)
        q = _dg(xn_s[...], wq_ref[...]).astype(BF16)
        for i in range(2 * heads):
            q_s[i] = q[:, i * nkeys:(i + 1) * nkeys]

        lane_tiles = [slice(lt * LANES, (lt + 1) * LANES) for lt in range(h_ref.shape[0] // LANES)]

        def head_body(hd, carry):
            s0_all = _dg(keys_ref[2 * hd], q_s[2 * hd], NT)
            s1_all = _dg(keys_ref[2 * hd + 1], q_s[2 * hd + 1], NT)
            nt = len(lane_tiles)
            s0_l = [s0_all[:, ls] for ls in lane_tiles]
            s1_l = [s1_all[:, ls] for ls in lane_tiles]
            tops, _ = _top_values_each(s0_l + s1_l, k, [False] * (2 * nt))
            assert k == 16
            cand_l = []
            for a, b in zip(tops[:nt], tops[nt:]):
                cand_l.append(jnp.concatenate(
                    [a[0:1, :] + b, a[1:2, :] + b[0:8, :]] + [a[i:i + 1, :] + b[0:8, :] for i in range(2, 8)]
                    + [a[8:16, :] + b[0:1, :]], axis=0))
            cv_l, _ = _top_values_each(cand_l, k, [False] * nt)
            for lt in range(nt):
                s0, s1, a, b, cv = s0_l[lt], s1_l[lt], tops[lt], tops[nt + lt], cv_l[lt]
                z = jnp.sum(jnp.exp(cv - cv[0:1, :]), axis=0, keepdims=True)
                e0_s[hd, lt] = jnp.exp(s0 - a[0:1, :]) * (1.0 / z)
                e1_s[hd, lt] = jnp.exp(s1 - b[0:1, :])
                eb1 = jnp.exp(b - b[0:1, :])
                tau_r = jnp.full(a.shape, jnp.inf, F32)
                for r in range(k):
                    tau_r = jnp.where(a + b[r:r + 1, :] >= cv[k - 1:k, :], eb1[r:r + 1, :], tau_r)
                tau = jnp.full(s0.shape, jnp.inf, F32)
                for r in reversed(range(k)):
                    tau = jnp.where(s0 >= a[r:r + 1, :], tau_r[r:r + 1, :], tau)
                tau_s[hd, lt] = tau
            return carry

        lax.fori_loop(0, heads, head_body, 0)
        acc_s[...] = jnp.zeros_like(acc_s)

    per = eb // nkeys
    assert per % 8 == 0
    i0 = pl.multiple_of(e * per, 8)
    nlt = ht_s.shape[0]
    mxw = nlt

    def pre_act(p):
        rows = slice(p * mxw * LANES, (p + 1) * mxw * LANES)
        res = _dg(u_ref[...], xn_s[rows, :], NT)
        for j in range(mxw):
            ht_s[p * mxw + j] = res[:, j * LANES:(j + 1) * LANES]

    def accumulate(p):
        cols = slice(p * mxw * LANES, (p + 1) * mxw * LANES)
        gp = jnp.concatenate([g_s[p * mxw + j] for j in range(mxw)], axis=1)
        acc_s[:, cols] += _dg(vt_ref[...], gp)

    def gate(lt):
        tau_l = [tau_s[hd, lt, pl.ds(i0, per), :] for hd in range(heads)]
        e0_l = [e0_s[hd, lt, pl.ds(i0, per), :] for hd in range(heads)]
        for il in range(per):
            w = None
            for hd in range(heads):
                e1 = e1_s[hd, lt]
                wh = jnp.where(e1 >= tau_l[hd][il:il + 1, :], e1, 0.0) * e0_l[hd][il:il + 1, :]
                w = wh if w is None else w + wh
            g = jax.nn.gelu(ht_s[lt, il * nkeys:(il + 1) * nkeys, :]) * w
            g_s[lt, il * nkeys:(il + 1) * nkeys, :] = g.astype(BF16)

    npieces = nlt // mxw
    for p in range(npieces):
        pre_act(p)
    for lt in range(nlt):
        gate(lt)
    for p in range(npieces):
        accumulate(p)

    @pl.when(e == ne - 1)
    def _():
        out = h_ref[...] + acc_s[...].T
        y_ref[...] = out * lax.rsqrt(jnp.mean(out * out, -1, keepdims=True) + NORM_EPS) * lnf_ref[...]


def _peer(h, ln2, wq, keys, u, vt, lnf, tb, eb=2048):
    m, d = h.shape
    nq = wq.shape[1]
    nk2, nkeys, half = keys.shape
    heads = nk2 // 2
    nexp = u.shape[0]
    ne = nexp // eb
    assert tb % LANES == 0
    nlt = tb // LANES
    kern = functools.partial(_peer_kernel, heads=heads, nkeys=nkeys, eb=eb, ne=ne)
    return pl.pallas_call(
        kern,
        grid=(m // tb, ne),
        in_specs=[
            pl.BlockSpec((tb, d), lambda t, e: (t, 0)),
            pl.BlockSpec((1, d), lambda t, e: (0, 0)),
            pl.BlockSpec((d, nq), lambda t, e: (0, 0)),
            pl.BlockSpec((nk2, nkeys, half), lambda t, e: (0, 0, 0)),
            pl.BlockSpec((eb, d), lambda t, e: (e, 0)),
            pl.BlockSpec((d, eb), lambda t, e: (0, e)),
            pl.BlockSpec((1, d), lambda t, e: (0, 0)),
        ],
        out_specs=pl.BlockSpec((tb, d), lambda t, e: (t, 0)),
        out_shape=jax.ShapeDtypeStruct((m, d), F32),
        scratch_shapes=[
            pltpu.VMEM((tb, d), BF16),
            pltpu.VMEM((nk2, tb, half), BF16),
            pltpu.VMEM((heads, nlt, nkeys, LANES), F32),
            pltpu.VMEM((heads, nlt, nkeys, LANES), F32),
            pltpu.VMEM((heads, nlt, nkeys, LANES), F32),
            pltpu.VMEM((nlt, eb, LANES), F32),
            pltpu.VMEM((nlt, eb, LANES), BF16),
            pltpu.VMEM((d, tb), F32),
        ],
        compiler_params=pltpu.CompilerParams(
            dimension_semantics=("arbitrary", "arbitrary"), vmem_limit_bytes=VMEM_LIMIT),
        name="peer",
    )(h, ln2, wq, keys, u, vt, lnf)


def _pad_lanes(v, width, offset=0):
    out = jnp.zeros((width,), F32)
    return out.at[offset:offset + v.shape[0]].set(v.astype(F32))


def kernel(x_prompt, x_sample, cache_dn_conv, state_dn, cache_rw_shift, state_rw, meta_tokens, ln1, w_in, dn_conv_w, dn_a_log, dn_dt_bias, dn_norm_w, rw_mu, rw_w0, rw_w2, rw_a0, rw_a2, rw_g2, rw_k_k, rw_k_a, rw_r_k, rw_gn_w, rw_gn_b, w_up_dn, w_up_rw, w_out, ln2, peer_wq, peer_keys, peer_u, peer_v, ln_f):
    bp, seq, d = x_prompt.shape
    bs, seq_s, _ = x_sample.shape
    depth = w_in.shape[0]
    assert depth == 1 and seq_s == 1
    n_meta = meta_tokens.shape[0]
    _, _, heads_dn, dk, dv = state_dn.shape
    _, _, heads_rw, n_rw, _ = state_rw.shape
    n_conv = dn_conv_w.shape[2]
    nqkv = 2 * heads_dn * dk + heads_dn * dv
    assert n_conv == nqkv == 1536 and dn_conv_w.shape[1] == 4 and heads_dn * dv == 512
    nw = heads_rw * n_rw
    rank_w, rank_a, rank_g = rw_w2.shape[1], rw_a2.shape[1], rw_g2.shape[1]
    assert nw == 512 and rank_w + rank_a == LANES and rank_g == LANES
    n_rwp = 3 * nw + rank_w + rank_a + rank_g
    o_z = nqkv
    o_beta = o_z + heads_dn * dv
    o_alpha = o_beta + heads_dn
    o_rw = o_alpha + heads_dn
    o_gate = o_rw + n_rwp
    assert w_in.shape[2] == o_gate + 2 * d and 2 * heads_dn == 8

    wi = w_in[0]
    seg1 = jnp.concatenate([wi[:, o_rw:o_gate], wi[:, o_beta:o_rw],
                            jnp.zeros((d, 2048 - n_rwp - 2 * heads_dn), F32)], axis=1)
    w_packed = jnp.concatenate([wi[:, 0:o_beta], seg1, wi[:, o_gate:]], axis=1).astype(BF16)
    w_bt = wi[:, o_beta:o_rw].T.astype(BF16)
    g1 = ln1[0][None, :]
    cw = dn_conv_w[0]
    prow = jnp.stack([_pad_lanes(dn_a_log[0], LANES, heads_dn), _pad_lanes(dn_dt_bias[0], LANES, heads_dn)])
    pcol = jnp.stack([_pad_lanes(dn_a_log[0], 8, heads_dn), _pad_lanes(dn_dt_bias[0], 8, heads_dn)], axis=1)
    nwd = dn_norm_w[0][None, :]
    mu = rw_mu[0]
    par = jnp.stack([mu[0:nw], mu[nw:2 * nw], mu[2 * nw:3 * nw], _pad_lanes(mu[3 * nw:], nw),
                     rw_w0[0], rw_a0[0], rw_k_k[0], rw_k_a[0]])
    wwa = jnp.zeros((LANES, 2 * nw), F32)
    wwa = wwa.at[0:rank_w, 0:nw].set(rw_w2[0]).at[rank_w:, nw:].set(rw_a2[0]).astype(BF16)
    g2 = rw_g2[0].astype(BF16)
    grp = jnp.arange(nw) // n_rw
    bd = (grp[:, None] == grp[None, :]).astype(BF16)
    epi = jnp.zeros((8, nw), F32).at[0].set(rw_gn_w[0]).at[1].set(rw_gn_b[0]).at[2].set(rw_r_k[0].reshape(-1))
    wd = w_up_dn[0].astype(BF16)
    wr = w_up_rw[0].astype(BF16)
    wo = w_out[0].astype(BF16)
    l2 = ln2[0][None, :]
    wq = peer_wq[0].astype(BF16)
    pk = peer_keys[0]
    keys = pk.reshape(pk.shape[0] * pk.shape[1], pk.shape[2], pk.shape[3]).astype(BF16)
    pu = peer_u[0].astype(BF16)
    pvt = peer_v[0].T.astype(BF16)
    lf = ln_f[None, :]

    assert n_meta <= ROWS and seq % ROWS == 0
    nreal = seq // ROWS
    x_meta = jnp.concatenate([jnp.zeros((ROWS - n_meta, d), F32), meta_tokens.astype(F32)], axis=0)
    xr = x_prompt.reshape(bp * seq, d)
    tm = max(t for t in (1024, 512, 256, ROWS) if (bp * seq) % t == 0)
    pm, batm = _norm_proj(x_meta, g1, w_packed, w_bt, ROWS)
    pp, batp = _norm_proj(xr, g1, w_packed, w_bt, tm)
    odn_p, p_dn = _dn_prompt(pp, batp, pm, batm, cw, prow, pcol, nwd, bp, nreal, heads_dn, dk, dv)
    orw_p, p_rw = _rw_prompt(pp, pm, par, wwa, g2, bd, epi, bp, nreal, heads_rw, n_rw, rank_w)
    tmm = max(t for t in (1024, 512, 256, ROWS) if (bp * seq) % t == 0)
    hp = _merge(xr, odn_p, orw_p, pp, wd, wr, wo, 1, bp * seq // tmm, bp * seq // tmm, 0, tmm)
    y_prompt = _peer(hp, l2, wq, keys, pu, pvt, lf, 512).reshape(bp, seq, d)
    pp3 = pp.reshape(bp, seq, -1)
    p_conv = pp3[:, seq - 3:, 0:nqkv]
    p_shift = pp3[:, seq - 1:, 2048:2048 + n_rwp]

    xs = x_sample.reshape(bs, d)
    ps, _ = _norm_proj(xs, g1, w_packed, w_bt, bs)
    cache_t = jnp.moveaxis(cache_dn_conv[0], 1, 0)
    odn_s, s_dn = _dn_sample(ps, cache_t, cw, prow, nwd, state_dn[0], heads_dn, dk, dv)
    orw_s, s_rw = _rw_sample(ps, cache_rw_shift[0][:, 0, :], par, wwa, g2, bd, epi, state_rw[0],
                             heads_rw, n_rw, rank_w)
    hs = _merge(xs, odn_s, orw_s, ps, wd, wr, wo, 1, 1, 1, 0, bs)
    y_sample = _peer(hs, l2, wq, keys, pu, pvt, lf, bs).reshape(bs, 1, d)
    s_conv = jnp.concatenate([cache_dn_conv[0][:, 1:], ps[:, None, 0:nqkv]], axis=1)
    s_shift = ps[:, None, 2048:2048 + n_rwp]

    return (y_prompt, y_sample, p_conv[None], p_dn[None], p_shift[None], p_rw[None],
            s_conv[None], s_dn[None], s_shift[None], s_rw[None])
```
